```python
import math
import jax
import jax.numpy as jnp
from jax import lax
import numpy as np

D_MODEL = 1024
BATCH = 8
SEQ = 4096
DEPTH = 4

A_HEADS = 6
A_KDIM = 128
A_VDIM = 64
A_CHUNK = 64
B_HEADS = 6
B_HDIM = 64
B_BLOCK = 128
C_GROUPS = ((128, 1), (512, 4), (2048, 16))
C_HEADS_PER_GROUP = 4
C_HEADS = C_HEADS_PER_GROUP * len(C_GROUPS)
C_HDIM = 64
C_BLOCK = 128
D_FF = 2816
N_BRANCH = 3
N_NORMS = 6
EPS = 1e-6
NEG_BIG = -1e30
TINY = 1e-30

A_QK = A_HEADS * A_KDIM
A_V = A_HEADS * A_VDIM
B_W = B_HEADS * B_HDIM
C_W = C_HEADS * C_HDIM
C_OUT = C_HEADS_PER_GROUP * C_HDIM
IN_COLS = 2 * A_QK + 2 * A_V + 3 * B_W + 3 * C_W + N_BRANCH * D_MODEL

kernel_name = 'hybrid_hgrn2_stickbreak_dilated_macaron'


def rms_norm(x, g):
    xf = x.astype(jnp.float32)
    y = xf * lax.rsqrt(jnp.mean(xf * xf, axis=-1, keepdims=True) + EPS)
    return (y * g.astype(jnp.float32)).astype(x.dtype)


def swiglu_ffn(h, w_in, w_out):
    a, b = jnp.split(h @ w_in, 2, axis=-1)
    return (jax.nn.silu(a) * b) @ w_out


def alibi_slopes(n):
    def pow2_slopes(m):
        start = 2.0 ** (-8.0 / m)
        return [start ** (i + 1) for i in range(m)]
    if math.log2(n).is_integer():
        s = pow2_slopes(n)
    else:
        c = 2 ** int(math.floor(math.log2(n)))
        s = pow2_slopes(c) + pow2_slopes(2 * c)[0::2][: n - c]
    return sorted(s, reverse=True)


def hgrn2_mixer(q, f_raw, i, g, lb, norm_g):
    f32 = jnp.float32
    bsz, seq, nh, kd = q.shape
    vd = i.shape[-1]
    lb = lb.astype(f32)
    f_raw = f_raw.astype(f32)
    qf = jax.nn.silu(q.astype(f32))
    f = lb + (1.0 - lb) * jax.nn.sigmoid(f_raw)
    log_f = jnp.log(jnp.maximum(f, TINY))
    kf = (1.0 - lb) * jax.nn.sigmoid(-f_raw)
    vf = i.astype(f32)
    n = seq // A_CHUNK

    def to_chunks(t):
        return t.reshape(bsz, n, A_CHUNK, nh, t.shape[-1]).transpose(1, 0, 3, 2, 4)

    causal = jnp.tril(jnp.ones((A_CHUNK, A_CHUNK), dtype=bool))

    def step(state, xs):
        qc, kc, vc, lfc = xs
        b = jnp.cumsum(lfc, axis=2)
        o_inter = jnp.einsum('bhtk,bhkv->bhtv', qc * jnp.exp(b), state)
        diff = b[:, :, :, None, :] - b[:, :, None, :, :]
        decay = jnp.exp(jnp.where(causal[:, :, None], diff, NEG_BIG))
        scores = jnp.einsum('bhtk,bhsk,bhtsk->bhts', qc, kc, decay)
        o_intra = jnp.einsum('bhts,bhsv->bhtv', scores, vc)
        b_end = b[:, :, -1:, :]
        new_state = (jnp.exp(b_end[:, :, 0, :])[..., None] * state
                     + jnp.einsum('bhsk,bhsv->bhkv', kc * jnp.exp(b_end - b), vc))
        return new_state, o_inter + o_intra

    state0 = jnp.zeros((bsz, nh, kd, vd), f32)
    _, o = lax.scan(step, state0, (to_chunks(qf), to_chunks(kf), to_chunks(vf), to_chunks(log_f)))
    o = o.transpose(1, 0, 3, 2, 4).reshape(bsz, seq, nh, vd)
    o = rms_norm(o, norm_g) * jax.nn.silu(g.astype(f32))
    return o.reshape(bsz, seq, nh * vd)


def stick_breaking_mixer(q, k, v):
    f32 = jnp.float32
    bsz, seq, nh, d = q.shape
    qh, kh, vh = (t.astype(f32).transpose(0, 2, 1, 3) for t in (q, k, v))
    nb = seq // B_BLOCK
    q_blocks = qh.reshape(bsz, nh, nb, B_BLOCK, d).transpose(2, 0, 1, 3, 4)
    starts = jnp.arange(nb) * B_BLOCK
    key_pos = jnp.arange(seq)
    scale = d ** -0.5

    def block(args):
        qb, t0 = args
        z = jnp.einsum('bhtd,bhsd->bhts', qb, kh) * scale
        causal = key_pos[None, :] < (t0 + jnp.arange(B_BLOCK))[:, None]
        log_1m = jnp.where(causal, jax.nn.log_sigmoid(-z), 0.0)
        suffix = lax.cumsum(log_1m, axis=3, reverse=True) - log_1m
        a = jnp.where(causal, jnp.exp(jnp.where(causal, jax.nn.log_sigmoid(z) + suffix, NEG_BIG)), 0.0)
        return jnp.einsum('bhts,bhsd->bhtd', a, vh)

    o = lax.map(block, (q_blocks, starts))
    return o.transpose(1, 0, 3, 2, 4).reshape(bsz, seq, nh * d)


def dilated_group_attention(q, k, v, window, dilation, slopes):
    f32 = jnp.float32
    bsz, seq, nh, d = q.shape
    sub_len = seq // dilation
    nb = -(-sub_len // C_BLOCK)
    lp = nb * C_BLOCK
    pad = lp - sub_len

    def strided(t):
        return t.astype(f32).reshape(bsz, sub_len, dilation, nh, d).transpose(0, 2, 3, 1, 4)

    qs = jnp.pad(strided(q), ((0, 0), (0, 0), (0, 0), (0, pad), (0, 0)))
    ks = jnp.pad(strided(k), ((0, 0), (0, 0), (0, 0), (C_BLOCK, pad), (0, 0)))
    vs = jnp.pad(strided(v), ((0, 0), (0, 0), (0, 0), (C_BLOCK, pad), (0, 0)))
    blk = (bsz, dilation, nh, nb, C_BLOCK, d)
    qb = qs.reshape(blk)
    kb = jnp.concatenate([ks[:, :, :, :lp].reshape(blk), ks[:, :, :, C_BLOCK:].reshape(blk)], axis=4)
    vb = jnp.concatenate([vs[:, :, :, :lp].reshape(blk), vs[:, :, :, C_BLOCK:].reshape(blk)], axis=4)
    s = jnp.einsum('brhnid,brhnjd->brhnij', qb, kb) * d ** -0.5
    qi = jnp.arange(C_BLOCK)[:, None]
    kj = jnp.arange(2 * C_BLOCK)[None, :]
    delta = qi + C_BLOCK - kj
    key_idx = (jnp.arange(nb) * C_BLOCK)[:, None, None] + kj[None] - C_BLOCK
    valid = (delta >= 0) & (delta <= window // dilation) & (key_idx >= 0)
    bias = -slopes[:, None, None, None] * (delta * dilation).astype(f32)
    s = jnp.where(valid, s + bias, NEG_BIG)
    m = jnp.max(s, axis=-1, keepdims=True)
    p = jnp.exp(s - m)
    den = jnp.sum(p, axis=-1, keepdims=True)
    o = jnp.einsum('brhnij,brhnjd->brhnid', p, vb) / den
    lse = (m + jnp.log(den))[..., 0]
    o = o.reshape(bsz, dilation, nh, lp, d)[:, :, :, :sub_len].transpose(0, 3, 1, 2, 4).reshape(bsz, seq, nh, d)
    lse = lse.reshape(bsz, dilation, nh, lp)[:, :, :, :sub_len].transpose(0, 3, 1, 2).reshape(bsz, seq, nh)
    return o, lse


def dilated_mixer(q, k, v):
    bsz, seq = q.shape[:2]
    slopes = jnp.asarray(alibi_slopes(C_HEADS), jnp.float32)
    outs, lses = [], []
    for gi, (w, r) in enumerate(C_GROUPS):
        sl = slice(gi * C_HEADS_PER_GROUP, (gi + 1) * C_HEADS_PER_GROUP)
        o, lse = dilated_group_attention(q[:, :, sl], k[:, :, sl], v[:, :, sl], w, r, slopes[sl])
        outs.append(o)
        lses.append(lse)
    wts = jax.nn.softmax(jnp.stack(lses, axis=2), axis=2)
    o = jnp.sum(wts[..., None] * jnp.stack(outs, axis=2), axis=2)
    return o.reshape(bsz, seq, C_OUT)


def token_mixing(h, w_in, lb, a_norm_g, w_br_a, w_br_b, w_br_c, w_out):
    bsz, seq, _ = h.shape
    widths = [A_QK, A_QK, A_V, A_V, B_W, B_W, B_W, C_W, C_W, C_W]
    idx = [int(v) for v in np.cumsum(widths)]
    aq, af, ai, ag, bq, bk, bv, cq, ck, cv, gate_logits = jnp.split(h @ w_in, idx, axis=-1)

    def heads(t, n):
        return t.reshape(bsz, seq, n, -1)

    ya = hgrn2_mixer(heads(aq, A_HEADS), heads(af, A_HEADS), heads(ai, A_HEADS), heads(ag, A_HEADS),
                     lb.reshape(A_HEADS, A_KDIM), a_norm_g)
    yb = stick_breaking_mixer(heads(bq, B_HEADS), heads(bk, B_HEADS), heads(bv, B_HEADS))
    yc = dilated_mixer(heads(cq, C_HEADS), heads(ck, C_HEADS), heads(cv, C_HEADS))
    gates = jax.nn.sigmoid(gate_logits.reshape(bsz, seq, N_BRANCH, D_MODEL).astype(jnp.float32)).astype(h.dtype)
    merged = (gates[:, :, 0] * (ya.astype(h.dtype) @ w_br_a)
              + gates[:, :, 1] * (yb.astype(h.dtype) @ w_br_b)
              + gates[:, :, 2] * (yc.astype(h.dtype) @ w_br_c))
    return merged @ w_out


def modulated_sublayer(x, fn, g_pre, g_post, m, res_w):
    shift, scale, gate = m[:, 0][:, None], m[:, 1][:, None], m[:, 2][:, None]
    h = rms_norm(x, g_pre) * (1.0 + scale) + shift
    return x + res_w * gate * rms_norm(fn(h), g_post)


def setup_inputs(seed: int = 0) -> dict:
    key = jax.random.key(seed)
    ks = jax.random.split(key, 18)
    nrm = jax.random.normal
    f32 = jnp.float32
    return {
        'x': nrm(ks[0], (BATCH, SEQ, D_MODEL), f32),
        'c': nrm(ks[1], (BATCH, D_MODEL), f32),
        'w_ada': nrm(ks[2], (DEPTH, D_MODEL, 9 * D_MODEL), f32) * (0.5 * D_MODEL ** -0.5),
        'b_ada': nrm(ks[3], (DEPTH, 9 * D_MODEL), f32) * 0.01,
        'norm_g': 1.0 + 0.05 * nrm(ks[4], (DEPTH, N_NORMS, D_MODEL), f32),
        'ffn1_w_in': nrm(ks[5], (DEPTH, D_MODEL, 2 * D_FF), f32) * D_MODEL ** -0.5,
        'ffn1_w_out': nrm(ks[6], (DEPTH, D_FF, D_MODEL), f32) * D_FF ** -0.5,
        'w_in': nrm(ks[7], (DEPTH, D_MODEL, IN_COLS), f32) * D_MODEL ** -0.5,
        'hgrn_lb_logits': nrm(ks[8], (DEPTH, A_QK), f32),
        'hgrn_norm_g': 1.0 + 0.05 * nrm(ks[9], (DEPTH, A_VDIM), f32),
        'w_branch_a': nrm(ks[10], (DEPTH, A_V, D_MODEL), f32) * A_V ** -0.5,
        'w_branch_b': nrm(ks[11], (DEPTH, B_W, D_MODEL), f32) * B_W ** -0.5,
        'w_branch_c': nrm(ks[12], (DEPTH, C_OUT, D_MODEL), f32) * C_OUT ** -0.5,
        'w_out': nrm(ks[13], (DEPTH, D_MODEL, D_MODEL), f32) * D_MODEL ** -0.5,
        'ffn2_w_in': nrm(ks[14], (DEPTH, D_MODEL, 2 * D_FF), f32) * D_MODEL ** -0.5,
        'ffn2_w_out': nrm(ks[15], (DEPTH, D_FF, D_MODEL), f32) * D_FF ** -0.5,
    }


def reference(x, c, w_ada, b_ada, norm_g, ffn1_w_in, ffn1_w_out, w_in, hgrn_lb_logits, hgrn_norm_g,
              w_branch_a, w_branch_b, w_branch_c, w_out, ffn2_w_in, ffn2_w_out):
    bsz = x.shape[0]
    lb_p = jax.nn.softmax(hgrn_lb_logits.astype(jnp.float32), axis=0)
    lb_all = jnp.cumsum(lb_p, axis=0) - lb_p[0:1]
    c_act = jax.nn.silu(c)
    for l in range(DEPTH):
        mod = (c_act @ w_ada[l] + b_ada[l]).reshape(bsz, 3, 3, D_MODEL)
        x = modulated_sublayer(x, lambda h: swiglu_ffn(h, ffn1_w_in[l], ffn1_w_out[l]),
                               norm_g[l, 0], norm_g[l, 1], mod[:, 0], 0.5)
        x = modulated_sublayer(x, lambda h: token_mixing(h, w_in[l], lb_all[l], hgrn_norm_g[l], w_branch_a[l],
                                                         w_branch_b[l], w_branch_c[l], w_out[l]),
                               norm_g[l, 2], norm_g[l, 3], mod[:, 1], 1.0)
        x = modulated_sublayer(x, lambda h: swiglu_ffn(h, ffn2_w_in[l], ffn2_w_out[l]),
                               norm_g[l, 4], norm_g[l, 5], mod[:, 2], 0.5)
    return x
```

```python
import functools
import math

import jax
import jax.numpy as jnp
from jax import lax
from jax.experimental import pallas as pl
from jax.experimental.pallas import tpu as pltpu

D_MODEL = 1024
DEPTH = 4
A_HEADS = 6
A_KDIM = 128
A_VDIM = 64
A_CHUNK = 64
B_HEADS = 6
B_HDIM = 64
B_BLOCK = 128
C_GROUPS = ((128, 1), (512, 4), (2048, 16))
C_HEADS_PER_GROUP = 4
C_HEADS = C_HEADS_PER_GROUP * len(C_GROUPS)
C_HDIM = 64
C_BLOCK = 128
D_FF = 2816
N_BRANCH = 3
EPS = 1e-6
NEG_BIG = -1e30
TINY = 1e-30

A_QK = A_HEADS * A_KDIM
A_V = A_HEADS * A_VDIM
B_W = B_HEADS * B_HDIM
C_W = C_HEADS * C_HDIM
C_OUT = C_HEADS_PER_GROUP * C_HDIM
MIX_COLS = 2 * A_QK + 2 * A_V + 3 * B_W + 3 * C_W
GATE_COLS = N_BRANCH * D_MODEL
IN_COLS = MIX_COLS + GATE_COLS

LANES = 128
OFF_AQ = 0
OFF_AF = A_QK // LANES
OFF_AI = 2 * A_QK // LANES
OFF_AG = (2 * A_QK + A_V) // LANES
OFF_BQ = (2 * A_QK + 2 * A_V) // LANES
OFF_BK = OFF_BQ + B_W // LANES
OFF_BV = OFF_BK + B_W // LANES
OFF_CQ = OFF_BV + B_W // LANES
OFF_CK = OFF_CQ + C_W // LANES
OFF_CV = OFF_CK + C_W // LANES
MIX_BLOCKS = MIX_COLS // LANES

VMEM_LIMIT = 56 * 1024 * 1024
SB_SKIP_LOG = -105.0
HGRN_SAFE_SPAN = 80.0

F32 = jnp.float32
BF16 = jnp.bfloat16


def _alibi_slopes(n):
    def pow2_slopes(m):
        start = 2.0 ** (-8.0 / m)
        return [start ** (i + 1) for i in range(m)]
    if math.log2(n).is_integer():
        s = pow2_slopes(n)
    else:
        c = 2 ** int(math.floor(math.log2(n)))
        s = pow2_slopes(c) + pow2_slopes(2 * c)[0::2][: n - c]
    return sorted(s, reverse=True)


def _rms(x, g):
    return x * lax.rsqrt(jnp.mean(x * x, axis=-1, keepdims=True) + EPS) * g


def _sigmoid(x):
    return 1.0 / (1.0 + jnp.exp(-x))


def _dot(a, b):
    return jnp.dot(a, b, preferred_element_type=F32)


def _dot_nt(a, b):
    return lax.dot_general(a, b, (((1,), (1,)), ((), ())), preferred_element_type=F32)


def _split_bf16(x):
    hi = x.astype(BF16)
    lo = (x - hi.astype(F32)).astype(BF16)
    return hi, lo


def _cparams(sem):
    return pltpu.CompilerParams(dimension_semantics=sem, vmem_limit_bytes=VMEM_LIMIT)


def _resident(shape):
    nd = len(shape)
    return pl.BlockSpec(shape, lambda *_: (0,) * nd, pipeline_mode=pl.Buffered(1))


def _ada_kernel(c_ref, w_ref, b_ref, o_ref):
    c = c_ref[...]
    ca = c * _sigmoid(c)
    o_ref[0] = jnp.dot(ca, w_ref[0], precision=lax.Precision.HIGHEST,
                       preferred_element_type=F32) + b_ref[0]


def _ada(c, w_ada, b_ada):
    depth, d, n = w_ada.shape
    bsz = c.shape[0]
    tn = 1152
    return pl.pallas_call(
        _ada_kernel,
        grid=(depth, n // tn),
        in_specs=[pl.BlockSpec((bsz, d), lambda l, j: (0, 0)),
                  pl.BlockSpec((1, d, tn), lambda l, j: (l, 0, j)),
                  pl.BlockSpec((1, 1, tn), lambda l, j: (l, 0, j))],
        out_specs=pl.BlockSpec((1, bsz, tn), lambda l, j: (l, 0, j)),
        out_shape=jax.ShapeDtypeStruct((depth, bsz, n), F32),
        compiler_params=_cparams(("parallel", "parallel")),
        name="ada_mod",
    )(c, w_ada, b_ada.reshape(depth, 1, n))


def _lb_kernel(l_ref, o_ref):
    x = l_ref[...]
    e = jnp.exp(x - jnp.max(x, axis=0, keepdims=True))
    p = e / jnp.sum(e, axis=0, keepdims=True)
    depth = x.shape[0]
    run = p[0:1]
    o_ref[0:1, :] = run - p[0:1]
    for l in range(1, depth):
        run = run + p[l:l + 1]
        o_ref[l:l + 1, :] = run - p[0:1]


def _lower_bounds(logits):
    return pl.pallas_call(_lb_kernel, out_shape=jax.ShapeDtypeStruct(logits.shape, F32),
                          name="hgrn_lower_bounds")(logits)


def _ffn_kernel(x_ref, mod_ref, g_ref, win_ref, wout_ref, o_ref, act_ref, *, res_w, ffc):
    x = x_ref[...]
    m = mod_ref[0]
    h = _rms(x, g_ref[0:1]) * (1.0 + m[1:2]) + m[0:1]
    hb = h.astype(BF16)
    d_ff = wout_ref.shape[0]
    for c in range(d_ff // ffc):
        a = _dot(hb, win_ref[:, c * ffc:(c + 1) * ffc])
        b = _dot(hb, win_ref[:, d_ff + c * ffc:d_ff + (c + 1) * ffc])
        act_ref[:, c * ffc:(c + 1) * ffc] = (a * _sigmoid(a) * b).astype(BF16)
    y = _dot(act_ref[...], wout_ref[...])
    o_ref[...] = x + res_w * m[2:3] * _rms(y, g_ref[1:2])


def _ffn_sublayer(x, mod, g2, w_in, w_out, res_w, seq, tm=512, ffc=256):
    t, d = x.shape
    d_ff = w_out.shape[0]
    per_b = seq // tm
    return pl.pallas_call(
        functools.partial(_ffn_kernel, res_w=res_w, ffc=ffc),
        grid=(t // tm,),
        in_specs=[pl.BlockSpec((tm, d), lambda i: (i, 0)),
                  pl.BlockSpec((1, 3, d), lambda i: (i // per_b, 0, 0)),
                  _resident((2, d)),
                  _resident((d, 2 * d_ff)),
                  _resident((d_ff, d))],
        out_specs=pl.BlockSpec((tm, d), lambda i: (i, 0)),
        out_shape=jax.ShapeDtypeStruct((t, d), F32),
        scratch_shapes=[pltpu.VMEM((tm, d_ff), BF16)],
        compiler_params=_cparams(("parallel",)),
        name="ffn_sublayer",
    )(x, mod, g2, w_in, w_out)


def _inproj_kernel(x_ref, mod_ref, g_ref, w_ref, mix_ref, gate_ref, *, cc):
    x = x_ref[...]
    m = mod_ref[0]
    h = _rms(x, g_ref[...]) * (1.0 + m[1:2]) + m[0:1]
    hb = h.astype(BF16)
    for c in range(MIX_COLS // cc):
        mix_ref[:, c * cc:(c + 1) * cc] = _dot(hb, w_ref[:, c * cc:(c + 1) * cc])
    for c in range(GATE_COLS // cc):
        z = _dot(hb, w_ref[:, MIX_COLS + c * cc:MIX_COLS + (c + 1) * cc])
        gate_ref[:, c * cc:(c + 1) * cc] = _sigmoid(z).astype(BF16)


def _inproj(x, mod, g, w_in, seq, tm=512, cc=384):
    t, d = x.shape
    per_b = seq // tm
    return pl.pallas_call(
        functools.partial(_inproj_kernel, cc=cc),
        grid=(t // tm,),
        in_specs=[pl.BlockSpec((tm, d), lambda i: (i, 0)),
                  pl.BlockSpec((1, 3, d), lambda i: (i // per_b, 0, 0)),
                  _resident((1, d)),
                  _resident((d, IN_COLS))],
        out_specs=[pl.BlockSpec((tm, MIX_COLS), lambda i: (i, 0)),
                   pl.BlockSpec((tm, GATE_COLS), lambda i: (i, 0))],
        out_shape=[jax.ShapeDtypeStruct((t, MIX_COLS), F32),
                   jax.ShapeDtypeStruct((t, GATE_COLS), BF16)],
        compiler_params=_cparams(("parallel",)),
        name="mix_inproj",
    )(x, mod, g, w_in)


def _hgrn_kernel(q_ref, f_ref, i_ref, g_ref, lb_ref, ng_ref, o_ref,
                 state_ref, qs_ref, ks_ref, vs_ref, bs_ref, oi_ref, *, ts):
    ck, kd, vd = A_CHUNK, A_KDIM, A_VDIM

    @pl.when(pl.program_id(2) == 0)
    def _():
        state_ref[...] = jnp.zeros_like(state_ref)

    row = lax.broadcasted_iota(jnp.int32, (ck, ck), 0)
    col = lax.broadcasted_iota(jnp.int32, (ck, ck), 1)
    tril = row >= col
    tri_incl = tril.astype(BF16)
    row1 = lax.broadcasted_iota(jnp.int32, (ck, 1), 0)
    ng = ng_ref[...]
    mid = ck // 2 - 1

    def chunk(c, carry):
        r0 = pl.multiple_of(c * ck, ck)
        rows = pl.ds(r0, ck)
        for j in range(2):
            ksl = slice(j * kd, (j + 1) * kd)
            vsl = slice(j * vd, (j + 1) * vd)
            qraw = q_ref[0, rows, ksl]
            fr = f_ref[0, rows, ksl]
            lb = lb_ref[:, ksl]
            q = qraw * _sigmoid(qraw)
            f = lb + (1.0 - lb) * _sigmoid(fr)
            lf = jnp.log(jnp.maximum(f, TINY))
            k = (1.0 - lb) * _sigmoid(-fr)
            v = i_ref[0, rows, vsl]
            vb = v.astype(BF16)
            lf_hi, lf_lo = _split_bf16(lf)
            b = _dot(tri_incl, lf_hi) + _dot(tri_incl, lf_lo)
            b_mid = b[mid:mid + 1]
            b_end = b[ck - 1:ck]
            state_t = state_ref[j]
            o_inter = _dot_nt((q * jnp.exp(b)).astype(BF16), state_t.astype(BF16))
            safe = jnp.logical_and(jnp.min(b_mid) > -HGRN_SAFE_SPAN,
                                   jnp.min(b_end - b_mid) > -HGRN_SAFE_SPAN)

            def fast():
                qa = (q * jnp.exp(b - b_mid)).astype(BF16)
                kb = (k * jnp.exp(b_mid - b)).astype(BF16)
                sc = jnp.where(tril, _dot_nt(qa, kb), 0.0)
                return _dot(sc.astype(BF16), vb)

            def slow():
                qs_ref[...] = q
                ks_ref[...] = k
                vs_ref[...] = v
                bs_ref[...] = b

                def one_row(t, cr):
                    bt = bs_ref[pl.ds(t, 1), :]
                    qt = qs_ref[pl.ds(t, 1), :]
                    dec = jnp.exp(jnp.where(row1 <= t, bt - bs_ref[...], NEG_BIG))
                    sc = jnp.sum(dec * qt * ks_ref[...], axis=-1, keepdims=True)
                    oi_ref[pl.ds(t, 1), :] = jnp.sum(sc * vs_ref[...], axis=0, keepdims=True)
                    return cr

                lax.fori_loop(0, ck, one_row, 0)
                return oi_ref[...]

            o_intra = lax.cond(safe, fast, slow)
            kd_end = (k * jnp.exp(b_end - b)).astype(BF16)
            upd_t = lax.dot_general(vb, kd_end, (((0,), (0,)), ((), ())), preferred_element_type=F32)
            state_ref[j] = jnp.exp(b_end) * state_t + upd_t
            o = o_inter + o_intra
            gate = g_ref[0, rows, vsl]
            o_ref[0, rows, vsl] = (_rms(o, ng) * (gate * _sigmoid(gate))).astype(o_ref.dtype)
        return carry

    lax.fori_loop(0, ts // ck, chunk, 0)


def _hgrn(mix3, lb, ng, ts=512):
    bsz, seq, _ = mix3.shape
    pairs = A_HEADS // 2
    kw, vw = 2 * A_KDIM, 2 * A_VDIM
    return pl.pallas_call(
        functools.partial(_hgrn_kernel, ts=ts),
        grid=(bsz, pairs, seq // ts),
        in_specs=[pl.BlockSpec((1, ts, kw), lambda b, p, s: (b, s, OFF_AQ * LANES // kw + p)),
                  pl.BlockSpec((1, ts, kw), lambda b, p, s: (b, s, OFF_AF * LANES // kw + p)),
                  pl.BlockSpec((1, ts, vw), lambda b, p, s: (b, s, OFF_AI + p)),
                  pl.BlockSpec((1, ts, vw), lambda b, p, s: (b, s, OFF_AG + p)),
                  pl.BlockSpec((1, kw), lambda b, p, s: (0, p)),
                  pl.BlockSpec((1, A_VDIM), lambda b, p, s: (0, 0))],
        out_specs=pl.BlockSpec((1, ts, vw), lambda b, p, s: (b, s, p)),
        out_shape=jax.ShapeDtypeStruct((bsz, seq, A_V), BF16),
        scratch_shapes=[pltpu.VMEM((2, A_VDIM, A_KDIM), F32),
                        pltpu.VMEM((A_CHUNK, A_KDIM), F32),
                        pltpu.VMEM((A_CHUNK, A_KDIM), F32),
                        pltpu.VMEM((A_CHUNK, A_VDIM), F32),
                        pltpu.VMEM((A_CHUNK, A_KDIM), F32),
                        pltpu.VMEM((A_CHUNK, A_VDIM), F32)],
        compiler_params=_cparams(("parallel", "parallel", "arbitrary")),
        name="hgrn2",
    )(mix3, mix3, mix3, mix3, lb, ng)


def _sb_kernel(q_ref, k_ref, v_ref, o_ref):
    blk, d = B_BLOCK, B_HDIM
    seq = q_ref.shape[1]
    scale = d ** -0.5
    row = lax.broadcasted_iota(jnp.int32, (blk, blk), 0)
    col = lax.broadcasted_iota(jnp.int32, (blk, blk), 1)
    later = (row > col).astype(BF16)

    def qblock(qb, carry0):
        q0 = pl.multiple_of(qb * blk, blk)
        for j in range(2):
            hs = slice(j * d, (j + 1) * d)
            q = (q_ref[0, pl.ds(q0, blk), hs] * scale).astype(BF16)

            def cond(st):
                kb, top, _, _ = st
                return jnp.logical_and(kb >= 0, top > SB_SKIP_LOG)

            def body(st):
                kb, _, run, acc = st
                k0 = pl.multiple_of(kb * blk, blk)
                k = k_ref[0, pl.ds(k0, blk), hs].astype(BF16)
                v = v_ref[0, pl.ds(k0, blk), hs].astype(BF16)
                z = _dot_nt(q, k)
                causal = (k0 + col) < (q0 + row)
                log_1m = jnp.where(causal, -(jnp.maximum(z, 0.0) + jnp.log1p(jnp.exp(-jnp.abs(z)))), 0.0)
                hi, lo = _split_bf16(log_1m)
                within = _dot(hi, later) + _dot(lo, later)
                a = jnp.where(causal, jnp.exp(z + log_1m + within + run), 0.0)
                acc = acc + _dot(a.astype(BF16), v)
                run = run + within[:, 0:1] + log_1m[:, 0:1]
                return kb - 1, jnp.max(run), run, acc

            st = lax.while_loop(cond, body, (qb, jnp.float32(0.0), jnp.zeros((blk, 1), F32),
                                             jnp.zeros((blk, d), F32)))
            o_ref[0, pl.ds(q0, blk), hs] = st[3].astype(o_ref.dtype)
        return carry0

    lax.fori_loop(0, seq // blk, qblock, 0)


def _stick_breaking(mix3):
    bsz, seq, _ = mix3.shape
    pairs = B_HEADS // 2
    w = 2 * B_HDIM
    return pl.pallas_call(
        _sb_kernel,
        grid=(bsz, pairs),
        in_specs=[pl.BlockSpec((1, seq, w), lambda b, p: (b, 0, OFF_BQ + p)),
                  pl.BlockSpec((1, seq, w), lambda b, p: (b, 0, OFF_BK + p)),
                  pl.BlockSpec((1, seq, w), lambda b, p: (b, 0, OFF_BV + p))],
        out_specs=pl.BlockSpec((1, seq, w), lambda b, p: (b, 0, p)),
        out_shape=jax.ShapeDtypeStruct((bsz, seq, B_W), BF16),
        compiler_params=_cparams(("parallel", "parallel")),
        name="stick_breaking",
    )(mix3, mix3, mix3)


def _dil_kernel(q_ref, k_ref, kp_ref, v_ref, vp_ref, o_ref, l_ref, *, rows, span, dilation, slopes):
    blk, d = C_BLOCK, C_HDIM
    pair = pl.program_id(2)
    n = pl.program_id(3)
    scale = d ** -0.5
    qi = lax.broadcasted_iota(jnp.int32, (blk, 2 * blk), 0)
    kj = lax.broadcasted_iota(jnp.int32, (blk, 2 * blk), 1)
    delta = qi + blk - kj
    window_ok = jnp.logical_and(delta >= 0, delta <= span)
    dist = (delta * dilation).astype(F32)
    for j in range(2):
        hs = slice(j * d, (j + 1) * d)
        slope = jnp.where(pair == 0, jnp.float32(slopes[j]), jnp.float32(slopes[2 + j]))
        for i in range(rows // blk):
            rs = slice(i * blk, (i + 1) * blk)
            q = (q_ref[0, rs, hs] * scale).astype(BF16)
            if i == 0:
                k_prev, v_prev = kp_ref[0, :, hs], vp_ref[0, :, hs]
                first = n == 0
                valid = jnp.logical_and(window_ok, jnp.logical_or(kj >= blk, jnp.logical_not(first)))
            else:
                ps = slice((i - 1) * blk, i * blk)
                k_prev, v_prev = k_ref[0, ps, hs], v_ref[0, ps, hs]
                valid = window_ok
            k2 = jnp.concatenate([k_prev, k_ref[0, rs, hs]], axis=0).astype(BF16)
            v2 = jnp.concatenate([v_prev, v_ref[0, rs, hs]], axis=0).astype(BF16)
            s = jnp.where(valid, _dot_nt(q, k2) - slope * dist, NEG_BIG)
            mx = jnp.max(s, axis=-1, keepdims=True)
            p = jnp.exp(s - mx)
            den = jnp.sum(p, axis=-1, keepdims=True)
            o_ref[0, rs, hs] = _dot(p.astype(BF16), v2) / den
            l_ref[0, rs, hs] = jnp.broadcast_to(mx + jnp.log(den), (blk, d))


def _dilated_group(mix3, gi):
    window, dilation = C_GROUPS[gi]
    bsz, seq, _ = mix3.shape
    sub = seq // dilation
    rows = min(512, sub)
    per = rows // C_BLOCK
    pairs = C_HEADS_PER_GROUP // 2
    w = 2 * C_HDIM
    view = mix3.reshape(bsz, sub, dilation * MIX_COLS)
    slopes = _alibi_slopes(C_HEADS)[gi * C_HEADS_PER_GROUP:(gi + 1) * C_HEADS_PER_GROUP]

    def cur(off):
        return pl.BlockSpec((1, rows, w), lambda b, r, p, n: (b, n, r * MIX_BLOCKS + off + pairs * gi + p))

    def prev(off):
        return pl.BlockSpec((1, C_BLOCK, w), lambda b, r, p, n: (b, jnp.maximum(n * per - 1, 0),
                                                                  r * MIX_BLOCKS + off + pairs * gi + p))

    out_spec = pl.BlockSpec((1, rows, w), lambda b, r, p, n: (b, n, r * pairs + p))
    out_shape = jax.ShapeDtypeStruct((bsz, sub, dilation * pairs * w), F32)
    o, lse = pl.pallas_call(
        functools.partial(_dil_kernel, rows=rows, span=window // dilation, dilation=dilation,
                          slopes=tuple(slopes)),
        grid=(bsz, dilation, pairs, sub // rows),
        in_specs=[cur(OFF_CQ), cur(OFF_CK), prev(OFF_CK), cur(OFF_CV), prev(OFF_CV)],
        out_specs=[out_spec, out_spec],
        out_shape=[out_shape, out_shape],
        compiler_params=_cparams(("parallel", "parallel", "parallel", "arbitrary")),
        name=f"dilated_g{gi}",
    )(view, view, view, view, view)
    return o.reshape(bsz * seq, C_OUT), lse.reshape(bsz * seq, C_OUT)


def _mixout_kernel(x_ref, mod_ref, g_ref, ya_ref, yb_ref, o0_ref, l0_ref, o1_ref, l1_ref, o2_ref, l2_ref,
                   gate_ref, wa_ref, wb_ref, wc_ref, wo_ref, out_ref):
    ls_ = [l0_ref[...], l1_ref[...], l2_ref[...]]
    mx = jnp.maximum(jnp.maximum(ls_[0], ls_[1]), ls_[2])
    ws = [jnp.exp(l - mx) for l in ls_]
    num = ws[0] * o0_ref[...] + ws[1] * o1_ref[...] + ws[2] * o2_ref[...]
    yc = (num / (ws[0] + ws[1] + ws[2])).astype(BF16)
    d = D_MODEL
    merged = gate_ref[:, 0:d].astype(F32) * _dot(ya_ref[...], wa_ref[...])
    merged = merged + gate_ref[:, d:2 * d].astype(F32) * _dot(yb_ref[...], wb_ref[...])
    merged = merged + gate_ref[:, 2 * d:3 * d].astype(F32) * _dot(yc, wc_ref[...])
    y = _dot(merged.astype(BF16), wo_ref[...])
    out_ref[...] = x_ref[...] + mod_ref[0][2:3] * _rms(y, g_ref[...])


def _mixout(x, mod, g, ya, yb, c_parts, gates, wa, wb, wc, wo, seq, tm=512):
    t, d = x.shape
    per_b = seq // tm

    def tok(wd):
        return pl.BlockSpec((tm, wd), lambda i: (i, 0))

    c_flat = [a for pair in c_parts for a in pair]
    return pl.pallas_call(
        _mixout_kernel,
        grid=(t // tm,),
        in_specs=[tok(d), pl.BlockSpec((1, 3, d), lambda i: (i // per_b, 0, 0)), _resident((1, d)),
                  tok(A_V), tok(B_W)] + [tok(C_OUT)] * len(c_flat)
                 + [tok(GATE_COLS), _resident(wa.shape), _resident(wb.shape), _resident(wc.shape),
                    _resident(wo.shape)],
        out_specs=tok(d),
        out_shape=jax.ShapeDtypeStruct((t, d), F32),
        compiler_params=_cparams(("parallel",)),
        name="mix_out",
    )(x, mod, g, ya, yb, *c_flat, gates, wa, wb, wc, wo)


def kernel(x, c, w_ada, b_ada, norm_g, ffn1_w_in, ffn1_w_out, w_in, hgrn_lb_logits, hgrn_norm_g,
           w_branch_a, w_branch_b, w_branch_c, w_out, ffn2_w_in, ffn2_w_out):
    bsz, seq, d = x.shape
    depth = w_ada.shape[0]
    lb_all = _lower_bounds(hgrn_lb_logits.astype(F32))
    mod = _ada(c, w_ada, b_ada).reshape(depth, bsz, 3, 3, d)
    xt = x.reshape(bsz * seq, d)
    for l in range(depth):
        xt = _ffn_sublayer(xt, mod[l, :, 0], norm_g[l, 0:2], ffn1_w_in[l].astype(BF16),
                           ffn1_w_out[l].astype(BF16), 0.5, seq)
        mix, gates = _inproj(xt, mod[l, :, 1], norm_g[l, 2:3], w_in[l].astype(BF16), seq)
        mix3 = mix.reshape(bsz, seq, MIX_COLS)
        ya = _hgrn(mix3, lb_all[l:l + 1], hgrn_norm_g[l:l + 1]).reshape(bsz * seq, A_V)
        yb = _stick_breaking(mix3).reshape(bsz * seq, B_W)
        c_parts = [_dilated_group(mix3, gi) for gi in range(len(C_GROUPS))]
        xt = _mixout(xt, mod[l, :, 1], norm_g[l, 3:4], ya, yb, c_parts, gates,
                     w_branch_a[l].astype(BF16), w_branch_b[l].astype(BF16), w_branch_c[l].astype(BF16),
                     w_out[l].astype(BF16), seq)
        xt = _ffn_sublayer(xt, mod[l, :, 2], norm_g[l, 4:6], ffn2_w_in[l].astype(BF16),
                           ffn2_w_out[l].astype(BF16), 0.5, seq)
    return xt.reshape(bsz, seq, d)
```

```python
import functools
import math

import jax
import jax.numpy as jnp
from jax import lax
from jax.experimental import pallas as pl
from jax.experimental.pallas import tpu as pltpu

D_MODEL = 1024
DEPTH = 4
A_HEADS = 6
A_KDIM = 128
A_VDIM = 64
A_CHUNK = 64
B_HEADS = 6
B_HDIM = 64
B_BLOCK = 128
C_GROUPS = ((128, 1), (512, 4), (2048, 16))
C_HEADS_PER_GROUP = 4
C_HEADS = C_HEADS_PER_GROUP * len(C_GROUPS)
C_HDIM = 64
C_BLOCK = 128
D_FF = 2816
N_BRANCH = 3
EPS = 1e-6
NEG_BIG = -1e30
TINY = 1e-30

A_QK = A_HEADS * A_KDIM
A_V = A_HEADS * A_VDIM
B_W = B_HEADS * B_HDIM
C_W = C_HEADS * C_HDIM
C_OUT = C_HEADS_PER_GROUP * C_HDIM
MIX_COLS = 2 * A_QK + 2 * A_V + 3 * B_W + 3 * C_W
GATE_COLS = N_BRANCH * D_MODEL
IN_COLS = MIX_COLS + GATE_COLS

LANES = 128
OFF_AQ = 0
OFF_AF = A_QK // LANES
OFF_AI = 2 * A_QK // LANES
OFF_AG = (2 * A_QK + A_V) // LANES
OFF_BQ = (2 * A_QK + 2 * A_V) // LANES
OFF_BK = OFF_BQ + B_W // LANES
OFF_BV = OFF_BK + B_W // LANES
OFF_CQ = OFF_BV + B_W // LANES
OFF_CK = OFF_CQ + C_W // LANES
OFF_CV = OFF_CK + C_W // LANES
MIX_BLOCKS = MIX_COLS // LANES

VMEM_LIMIT = 56 * 1024 * 1024
SB_SKIP_LOG = -105.0
SB_FIXED_BLOCKS = 3
DIL_BATCH = 4
HGRN_SAFE_SPAN = 80.0

F32 = jnp.float32
BF16 = jnp.bfloat16


def _alibi_slopes(n):
    def pow2_slopes(m):
        start = 2.0 ** (-8.0 / m)
        return [start ** (i + 1) for i in range(m)]
    if math.log2(n).is_integer():
        s = pow2_slopes(n)
    else:
        c = 2 ** int(math.floor(math.log2(n)))
        s = pow2_slopes(c) + pow2_slopes(2 * c)[0::2][: n - c]
    return sorted(s, reverse=True)


def _rms(x, g):
    return x * lax.rsqrt(jnp.mean(x * x, axis=-1, keepdims=True) + EPS) * g


def _sigmoid(x):
    return 1.0 / (1.0 + jnp.exp(-x))


def _dot(a, b):
    return jnp.dot(a, b, preferred_element_type=F32)


def _dot_nt(a, b):
    return lax.dot_general(a, b, (((1,), (1,)), ((), ())), preferred_element_type=F32)


def _split_bf16(x):
    hi = x.astype(BF16)
    lo = (x - hi.astype(F32)).astype(BF16)
    return hi, lo


def _cparams(sem):
    return pltpu.CompilerParams(dimension_semantics=sem, vmem_limit_bytes=VMEM_LIMIT)


def _resident(shape):
    nd = len(shape)
    return pl.BlockSpec(shape, lambda *_: (0,) * nd, pipeline_mode=pl.Buffered(1))


def _ada_kernel(c_ref, w_ref, b_ref, o_ref):
    c = c_ref[...]
    ca = c * _sigmoid(c)
    o_ref[0] = jnp.dot(ca, w_ref[0], precision=lax.Precision.HIGHEST,
                       preferred_element_type=F32) + b_ref[0]


def _ada(c, w_ada, b_ada):
    depth, d, n = w_ada.shape
    bsz = c.shape[0]
    tn = 1152
    return pl.pallas_call(
        _ada_kernel,
        grid=(depth, n // tn),
        in_specs=[pl.BlockSpec((bsz, d), lambda l, j: (0, 0)),
                  pl.BlockSpec((1, d, tn), lambda l, j: (l, 0, j)),
                  pl.BlockSpec((1, 1, tn), lambda l, j: (l, 0, j))],
        out_specs=pl.BlockSpec((1, bsz, tn), lambda l, j: (l, 0, j)),
        out_shape=jax.ShapeDtypeStruct((depth, bsz, n), F32),
        compiler_params=_cparams(("parallel", "parallel")),
        name="ada_mod",
    )(c, w_ada, b_ada.reshape(depth, 1, n))


def _lb_kernel(l_ref, o_ref):
    x = l_ref[...]
    e = jnp.exp(x - jnp.max(x, axis=0, keepdims=True))
    p = e / jnp.sum(e, axis=0, keepdims=True)
    depth = x.shape[0]
    run = p[0:1]
    o_ref[0:1, :] = run - p[0:1]
    for l in range(1, depth):
        run = run + p[l:l + 1]
        o_ref[l:l + 1, :] = run - p[0:1]


def _lower_bounds(logits):
    return pl.pallas_call(_lb_kernel, out_shape=jax.ShapeDtypeStruct(logits.shape, F32),
                          name="hgrn_lower_bounds")(logits)


def _ffn_kernel(x_ref, mod_ref, g_ref, win_ref, wout_ref, o_ref, act_ref, *, res_w, ffc):
    x = x_ref[...]
    m = mod_ref[0]
    h = _rms(x, g_ref[0:1]) * (1.0 + m[1:2]) + m[0:1]
    hb = h.astype(BF16)
    d_ff = wout_ref.shape[0]
    for c in range(d_ff // ffc):
        a = _dot(hb, win_ref[:, c * ffc:(c + 1) * ffc])
        b = _dot(hb, win_ref[:, d_ff + c * ffc:d_ff + (c + 1) * ffc])
        act_ref[:, c * ffc:(c + 1) * ffc] = (a * _sigmoid(a) * b).astype(BF16)
    y = _dot(act_ref[...], wout_ref[...])
    o_ref[...] = x + res_w * m[2:3] * _rms(y, g_ref[1:2])


def _ffn_sublayer(x, mod, g2, w_in, w_out, res_w, seq, tm=512, ffc=256):
    t, d = x.shape
    d_ff = w_out.shape[0]
    per_b = seq // tm
    return pl.pallas_call(
        functools.partial(_ffn_kernel, res_w=res_w, ffc=ffc),
        grid=(t // tm,),
        in_specs=[pl.BlockSpec((tm, d), lambda i: (i, 0)),
                  pl.BlockSpec((1, 3, d), lambda i: (i // per_b, 0, 0)),
                  _resident((2, d)),
                  _resident((d, 2 * d_ff)),
                  _resident((d_ff, d))],
        out_specs=pl.BlockSpec((tm, d), lambda i: (i, 0)),
        out_shape=jax.ShapeDtypeStruct((t, d), F32),
        scratch_shapes=[pltpu.VMEM((tm, d_ff), BF16)],
        compiler_params=_cparams(("parallel",)),
        name="ffn_sublayer",
    )(x, mod, g2, w_in, w_out)


def _inproj_kernel(x_ref, mod_ref, g_ref, w_ref, mix_ref, gate_ref, *, cc):
    x = x_ref[...]
    m = mod_ref[0]
    h = _rms(x, g_ref[...]) * (1.0 + m[1:2]) + m[0:1]
    hb = h.astype(BF16)
    for c in range(MIX_COLS // cc):
        mix_ref[:, c * cc:(c + 1) * cc] = _dot(hb, w_ref[:, c * cc:(c + 1) * cc])
    for c in range(GATE_COLS // cc):
        z = _dot(hb, w_ref[:, MIX_COLS + c * cc:MIX_COLS + (c + 1) * cc])
        gate_ref[:, c * cc:(c + 1) * cc] = _sigmoid(z).astype(BF16)


def _inproj(x, mod, g, w_in, seq, tm=512, cc=384):
    t, d = x.shape
    per_b = seq // tm
    return pl.pallas_call(
        functools.partial(_inproj_kernel, cc=cc),
        grid=(t // tm,),
        in_specs=[pl.BlockSpec((tm, d), lambda i: (i, 0)),
                  pl.BlockSpec((1, 3, d), lambda i: (i // per_b, 0, 0)),
                  _resident((1, d)),
                  _resident((d, IN_COLS))],
        out_specs=[pl.BlockSpec((tm, MIX_COLS), lambda i: (i, 0)),
                   pl.BlockSpec((tm, GATE_COLS), lambda i: (i, 0))],
        out_shape=[jax.ShapeDtypeStruct((t, MIX_COLS), F32),
                   jax.ShapeDtypeStruct((t, GATE_COLS), BF16)],
        compiler_params=_cparams(("parallel",)),
        name="mix_inproj",
    )(x, mod, g, w_in)


def _hgrn_kernel(q_ref, f_ref, i_ref, g_ref, lb_ref, ng_ref, o_ref,
                 state_ref, qs_ref, ks_ref, bs_ref, oi_ref, *, ts):
    ck, kd, vd = A_CHUNK, A_KDIM, A_VDIM
    nc = ts // ck
    mid = ck // 2 - 1

    @pl.when(pl.program_id(2) == 0)
    def _():
        state_ref[...] = jnp.zeros_like(state_ref)

    row = lax.broadcasted_iota(jnp.int32, (ck, ck), 0)
    col = lax.broadcasted_iota(jnp.int32, (ck, ck), 1)
    tril = row >= col
    tri_incl = tril.astype(BF16)
    row1 = lax.broadcasted_iota(jnp.int32, (ck, 1), 0)
    ng = ng_ref[...]

    def finish(o, rows, j):
        gate = g_ref[0, rows, j * vd:(j + 1) * vd]
        return _rms(o, ng) * (gate * _sigmoid(gate))

    heads = []
    worst = None
    for j in range(2):
        ksl = slice(j * kd, (j + 1) * kd)
        qraw = q_ref[0, :, ksl]
        fr = f_ref[0, :, ksl]
        lb = lb_ref[:, ksl]
        qs = qraw * _sigmoid(qraw)
        e = jnp.exp(-jnp.abs(fr))
        big = 1.0 / (1.0 + e)
        small = e * big
        pos = fr >= 0.0
        f = lb + (1.0 - lb) * jnp.where(pos, big, small)
        k = (1.0 - lb) * jnp.where(pos, small, big)
        lf_hi, lf_lo = _split_bf16(jnp.log(jnp.maximum(f, TINY)))
        b = jnp.concatenate(
            [_dot(tri_incl, lf_hi[c * ck:(c + 1) * ck]) + _dot(tri_incl, lf_lo[c * ck:(c + 1) * ck])
             for c in range(nc)], axis=0)
        heads.append((qs, k, b))
        for c in range(nc):
            b_mid = b[c * ck + mid:c * ck + mid + 1]
            b_end = b[(c + 1) * ck - 1:(c + 1) * ck]
            w = jnp.minimum(b_mid, b_end - b_mid)
            worst = w if worst is None else jnp.minimum(worst, w)
    safe = jnp.min(worst) > -HGRN_SAFE_SPAN

    @pl.when(safe)
    def _():
        units = [(c, j) for c in range(nc) for j in range(2)]
        rows_of = lambda c: slice(c * ck, (c + 1) * ck)
        qa, ka, q_dec, k_end, end_decay, vb = {}, {}, {}, {}, {}, {}
        for c, j in units:
            qs, k, b = heads[j]
            bc = b[rows_of(c)]
            b_mid = bc[mid:mid + 1]
            b_end = bc[ck - 1:ck]
            qa_f = qs[rows_of(c)] * jnp.exp(bc - b_mid)
            ka_f = k[rows_of(c)] * jnp.exp(b_mid - bc)
            qa[c, j], ka[c, j] = qa_f.astype(BF16), ka_f.astype(BF16)
            q_dec[c, j] = (qa_f * jnp.exp(b_mid)).astype(BF16)
            k_end[c, j] = (ka_f * jnp.exp(b_end - b_mid)).astype(BF16)
            end_decay[c, j] = jnp.exp(b_end)
            vb[c, j] = i_ref[0, rows_of(c), j * vd:(j + 1) * vd].astype(BF16)
        scores = {u: jnp.where(tril, _dot_nt(qa[u], ka[u]), 0.0).astype(BF16) for u in units}
        o_intra = {u: _dot(scores[u], vb[u]) for u in units}
        update = {u: lax.dot_general(vb[u], k_end[u], (((0,), (0,)), ((), ())), preferred_element_type=F32)
                  for u in units}
        state_before = {}
        for j in range(2):
            st = state_ref[j]
            for c in range(nc):
                state_before[c, j] = st.astype(BF16)
                st = end_decay[c, j] * st + update[c, j]
            state_ref[j] = st
        for c in range(nc):
            outs = [finish(o_intra[c, j] + _dot_nt(q_dec[c, j], state_before[c, j]), rows_of(c), j)
                    for j in range(2)]
            o_ref[0, rows_of(c), :] = jnp.concatenate(outs, axis=1).astype(o_ref.dtype)

    @pl.when(jnp.logical_not(safe))
    def _():
        for j in range(2):
            qs, k, b = heads[j]
            qs_ref[...] = qs
            ks_ref[...] = k
            bs_ref[...] = b
            vsl = slice(j * vd, (j + 1) * vd)

            def chunk(c, carry):
                r0 = pl.multiple_of(c * ck, ck)
                rows = pl.ds(r0, ck)
                bc = bs_ref[rows, :]
                qc = qs_ref[rows, :]
                kc = ks_ref[rows, :]
                v = i_ref[0, rows, vsl]
                vb = v.astype(BF16)

                def one_row(t, cr):
                    bt = bs_ref[pl.ds(r0 + t, 1), :]
                    qt = qs_ref[pl.ds(r0 + t, 1), :]
                    dec = jnp.exp(jnp.where(row1 <= t, bt - bc, NEG_BIG))
                    sc = jnp.sum(dec * qt * kc, axis=-1, keepdims=True)
                    oi_ref[pl.ds(t, 1), :] = jnp.sum(sc * v, axis=0, keepdims=True)
                    return cr

                lax.fori_loop(0, ck, one_row, 0)
                b_end = bc[ck - 1:ck]
                st = state_ref[j]
                o = oi_ref[...] + _dot_nt((qc * jnp.exp(bc)).astype(BF16), st.astype(BF16))
                k_end = (kc * jnp.exp(b_end - bc)).astype(BF16)
                state_ref[j] = jnp.exp(b_end) * st + lax.dot_general(
                    vb, k_end, (((0,), (0,)), ((), ())), preferred_element_type=F32)
                o_ref[0, rows, vsl] = finish(o, rows, j).astype(o_ref.dtype)
                return carry

            lax.fori_loop(0, nc, chunk, 0)


def _hgrn(mix3, lb, ng, ts=512):
    bsz, seq, _ = mix3.shape
    pairs = A_HEADS // 2
    kw, vw = 2 * A_KDIM, 2 * A_VDIM
    return pl.pallas_call(
        functools.partial(_hgrn_kernel, ts=ts),
        grid=(bsz, pairs, seq // ts),
        in_specs=[pl.BlockSpec((1, ts, kw), lambda b, p, s: (b, s, OFF_AQ * LANES // kw + p)),
                  pl.BlockSpec((1, ts, kw), lambda b, p, s: (b, s, OFF_AF * LANES // kw + p)),
                  pl.BlockSpec((1, ts, vw), lambda b, p, s: (b, s, OFF_AI + p)),
                  pl.BlockSpec((1, ts, vw), lambda b, p, s: (b, s, OFF_AG + p)),
                  pl.BlockSpec((1, kw), lambda b, p, s: (0, p)),
                  pl.BlockSpec((1, A_VDIM), lambda b, p, s: (0, 0))],
        out_specs=pl.BlockSpec((1, ts, vw), lambda b, p, s: (b, s, p)),
        out_shape=jax.ShapeDtypeStruct((bsz, seq, A_V), BF16),
        scratch_shapes=[pltpu.VMEM((2, A_VDIM, A_KDIM), F32),
                        pltpu.VMEM((ts, A_KDIM), F32),
                        pltpu.VMEM((ts, A_KDIM), F32),
                        pltpu.VMEM((ts, A_KDIM), F32),
                        pltpu.VMEM((A_CHUNK, A_VDIM), F32)],
        compiler_params=_cparams(("parallel", "parallel", "arbitrary")),
        name="hgrn2",
    )(mix3, mix3, mix3, mix3, lb, ng)


def _sb_kernel(q_ref, k_ref, v_ref, o_ref, run_ref, acc_ref):
    blk, d = B_BLOCK, B_HDIM
    seq = q_ref.shape[1]
    neg_scale = -(d ** -0.5)
    row = lax.broadcasted_iota(jnp.int32, (blk, blk), 0)
    col = lax.broadcasted_iota(jnp.int32, (blk, blk), 1)
    strictly_before = col < row
    r2 = lax.broadcasted_iota(jnp.int32, (blk, 2 * blk), 0)
    c2 = lax.broadcasted_iota(jnp.int32, (blk, 2 * blk), 1)
    later_and_ones = jnp.logical_or(r2 > c2, c2 >= blk).astype(BF16)

    def load_neg_q(q0, j):
        return (q_ref[0, pl.ds(q0, blk), j * d:(j + 1) * d] * neg_scale).astype(BF16)

    def logs(neg_q, k):
        zn = _dot_nt(neg_q, k)
        log_1m = jnp.minimum(zn, 0.0) - jnp.log(1.0 + jnp.exp(-jnp.abs(zn)))
        return log_1m, log_1m - zn

    def later_sums(log_1m):
        hi, lo = _split_bf16(log_1m)
        r = _dot(hi, later_and_ones) + _dot(lo, later_and_ones)
        return r[:, :blk], r[:, blk:]

    def qblock(qb, carry0):
        q0 = pl.multiple_of(qb * blk, blk)
        units = [(j, step) for j in range(2) for step in range(SB_FIXED_BLOCKS)]
        keeps, vals, neg_z = {}, {}, {}
        for j, step in units:
            hs = slice(j * d, (j + 1) * d)
            kb = qb - step
            k0 = pl.multiple_of(jnp.maximum(kb, 0) * blk, blk)
            keeps[j, step] = strictly_before if step == 0 else kb >= 0
            vals[j, step] = v_ref[0, pl.ds(k0, blk), hs].astype(BF16)
            neg_z[j, step] = _dot_nt(load_neg_q(q0, j), k_ref[0, pl.ds(k0, blk), hs].astype(BF16))
        log_bs, sums = {}, {}
        for u in units:
            zn = neg_z[u]
            log_1m = jnp.minimum(zn, 0.0) - jnp.log(1.0 + jnp.exp(-jnp.abs(zn)))
            log_bs[u] = log_1m - zn
            sums[u] = _split_bf16(jnp.where(keeps[u], log_1m, 0.0))
        for u in units:
            hi, lo = sums[u]
            sums[u] = _dot(hi, later_and_ones) + _dot(lo, later_and_ones)
        weights = {}
        tops = []
        for j in range(2):
            run = jnp.zeros((blk, blk), F32)
            for step in range(SB_FIXED_BLOCKS):
                u = (j, step)
                within, total = sums[u][:, :blk], sums[u][:, blk:]
                weights[u] = jnp.where(keeps[u], jnp.exp(log_bs[u] + within + run), 0.0).astype(BF16)
                run = run + total
            run_ref[j] = run
            tops.append(jnp.max(run))
        for j in range(2):
            acc = _dot(weights[j, 0], vals[j, 0])
            for step in range(1, SB_FIXED_BLOCKS):
                acc = acc + _dot(weights[j, step], vals[j, step])
            acc_ref[j] = acc

        def cond(st):
            kb, top = st
            return jnp.logical_and(kb >= 0, top > SB_SKIP_LOG)

        def body(st):
            kb, _ = st
            k0 = pl.multiple_of(kb * blk, blk)
            new_tops = []
            for j in range(2):
                hs = slice(j * d, (j + 1) * d)
                k = k_ref[0, pl.ds(k0, blk), hs].astype(BF16)
                v = v_ref[0, pl.ds(k0, blk), hs].astype(BF16)
                log_1m, log_b = logs(load_neg_q(q0, j), k)
                within, total = later_sums(log_1m)
                run = run_ref[j]
                a = jnp.exp(log_b + within + run)
                acc_ref[j] = acc_ref[j] + _dot(a.astype(BF16), v)
                run = run + total
                run_ref[j] = run
                new_tops.append(jnp.max(run))
            return kb - 1, jnp.maximum(new_tops[0], new_tops[1])

        lax.while_loop(cond, body, (qb - SB_FIXED_BLOCKS, jnp.maximum(tops[0], tops[1])))
        o_ref[0, pl.ds(q0, blk), :] = jnp.concatenate([acc_ref[0], acc_ref[1]], axis=1).astype(o_ref.dtype)
        return carry0

    lax.fori_loop(0, seq // blk, qblock, 0)


def _stick_breaking(mix3):
    bsz, seq, _ = mix3.shape
    pairs = B_HEADS // 2
    w = 2 * B_HDIM
    return pl.pallas_call(
        _sb_kernel,
        grid=(bsz, pairs),
        in_specs=[pl.BlockSpec((1, seq, w), lambda b, p: (b, 0, OFF_BQ + p)),
                  pl.BlockSpec((1, seq, w), lambda b, p: (b, 0, OFF_BK + p)),
                  pl.BlockSpec((1, seq, w), lambda b, p: (b, 0, OFF_BV + p))],
        out_specs=pl.BlockSpec((1, seq, w), lambda b, p: (b, 0, p)),
        out_shape=jax.ShapeDtypeStruct((bsz, seq, B_W), BF16),
        scratch_shapes=[pltpu.VMEM((2, B_BLOCK, B_BLOCK), F32),
                        pltpu.VMEM((2, B_BLOCK, B_HDIM), F32)],
        compiler_params=_cparams(("parallel", "parallel")),
        name="stick_breaking",
    )(mix3, mix3, mix3)


def _dil_kernel(*refs, tile, slopes):
    blk, d = C_BLOCK, C_HDIM
    n_groups = len(C_GROUPS)
    o_ref, os_ref, ls_ref = refs[5 * n_groups:]
    pair = pl.program_id(1)
    first_tile = pl.program_id(2) == 0
    scale = d ** -0.5
    qi = lax.broadcasted_iota(jnp.int32, (blk, 2 * blk), 0)
    kj = lax.broadcasted_iota(jnp.int32, (blk, 2 * blk), 1)
    delta = qi + blk - kj
    for g, (window, dil) in enumerate(C_GROUPS):
        q_ref, k_ref, kp_ref, v_ref, vp_ref = refs[5 * g:5 * g + 5]
        window_ok = jnp.logical_and(delta >= 0, delta <= window // dil)
        window_ok_first = jnp.logical_and(window_ok, jnp.logical_or(kj >= blk, jnp.logical_not(first_tile)))
        dist = (delta * dil).astype(F32)
        head_slopes = [jnp.where(pair == 0, jnp.float32(slopes[g][j]), jnp.float32(slopes[g][2 + j]))
                       for j in range(2)]
        def strided(ref, rho, block):
            start = rho + block * blk * dil
            return ref[0, pl.ds(start, blk, stride=dil), :] if dil > 1 else ref[0, pl.ds(start, blk), :]

        blocks = [(rho, i) for rho in range(dil) for i in range(tile // (dil * blk))]
        for at in range(0, len(blocks), DIL_BATCH):
            batch = blocks[at:at + DIL_BATCH]
            units = [(rho, i, j) for rho, i in batch for j in range(2)]
            scores, values = {}, {}
            for rho, i in batch:
                q_both = strided(q_ref, rho, i)
                k_both = jnp.concatenate([strided(kp_ref, rho, 0) if i == 0 else strided(k_ref, rho, i - 1),
                                          strided(k_ref, rho, i)], axis=0)
                v_both = jnp.concatenate([strided(vp_ref, rho, 0) if i == 0 else strided(v_ref, rho, i - 1),
                                          strided(v_ref, rho, i)], axis=0)
                for j in range(2):
                    hs = slice(j * d, (j + 1) * d)
                    scores[rho, i, j] = _dot_nt((q_both[:, hs] * scale).astype(BF16), k_both[:, hs].astype(BF16))
                    values[rho, i, j] = v_both[:, hs].astype(BF16)
            probs, dens, lses = {}, {}, {}
            for rho, i, j in units:
                valid = window_ok_first if i == 0 else window_ok
                s = jnp.where(valid, scores[rho, i, j] - head_slopes[j] * dist, NEG_BIG)
                mx = jnp.max(s, axis=-1, keepdims=True)
                p = jnp.exp(s - mx)
                dens[rho, i, j] = jnp.sum(p, axis=-1, keepdims=True)
                probs[rho, i, j] = p.astype(BF16)
                lses[rho, i, j] = jnp.broadcast_to(mx + jnp.log(dens[rho, i, j]), (blk, d))
            outs = {u: _dot(probs[u], values[u]) / dens[u] for u in units}
            for rho, i in batch:
                start = rho + i * blk * dil
                rows = pl.ds(start, blk, stride=dil) if dil > 1 else pl.ds(start, blk)
                os_ref[g, rows, :] = jnp.concatenate([outs[rho, i, 0], outs[rho, i, 1]], axis=1)
                ls_ref[g, rows, :] = jnp.concatenate([lses[rho, i, 0], lses[rho, i, 1]], axis=1)
    lses = [ls_ref[g] for g in range(n_groups)]
    top = functools.reduce(jnp.maximum, lses)
    wts = [jnp.exp(l - top) for l in lses]
    num = functools.reduce(lambda a, b: a + b, [wts[g] * os_ref[g] for g in range(n_groups)])
    o_ref[0] = (num / functools.reduce(lambda a, b: a + b, wts)).astype(o_ref.dtype)


def _dilated(mix3):
    bsz, seq, _ = mix3.shape
    pairs = C_HEADS_PER_GROUP // 2
    w = 2 * C_HDIM
    tile = C_BLOCK * max(dil for _, dil in C_GROUPS)
    all_slopes = _alibi_slopes(C_HEADS)
    slopes = tuple(tuple(all_slopes[g * C_HEADS_PER_GROUP:(g + 1) * C_HEADS_PER_GROUP])
                   for g in range(len(C_GROUPS)))
    in_specs = []
    for g, (_, dil) in enumerate(C_GROUPS):
        back = C_BLOCK * dil
        per = tile // back

        def cur(off, g=g):
            return pl.BlockSpec((1, tile, w), lambda b, p, n: (b, n, off + pairs * g + p))

        def prev(off, g=g, back=back, per=per):
            return pl.BlockSpec((1, back, w), lambda b, p, n: (b, jnp.maximum(n * per - 1, 0), off + pairs * g + p))

        in_specs += [cur(OFF_CQ), cur(OFF_CK), prev(OFF_CK), cur(OFF_CV), prev(OFF_CV)]
    return pl.pallas_call(
        functools.partial(_dil_kernel, tile=tile, slopes=slopes),
        grid=(bsz, pairs, seq // tile),
        in_specs=in_specs,
        out_specs=pl.BlockSpec((1, tile, w), lambda b, p, n: (b, n, p)),
        out_shape=jax.ShapeDtypeStruct((bsz, seq, C_OUT), BF16),
        scratch_shapes=[pltpu.VMEM((len(C_GROUPS), tile, w), F32),
                        pltpu.VMEM((len(C_GROUPS), tile, w), F32)],
        compiler_params=_cparams(("parallel", "parallel", "arbitrary")),
        name="dilated",
    )(*([mix3] * (5 * len(C_GROUPS))))


def _mixout_kernel(x_ref, mod_ref, g_ref, ya_ref, yb_ref, yc_ref, gate_ref, wa_ref, wb_ref, wc_ref, wo_ref,
                   out_ref):
    d = D_MODEL
    merged = gate_ref[:, 0:d].astype(F32) * _dot(ya_ref[...], wa_ref[...])
    merged = merged + gate_ref[:, d:2 * d].astype(F32) * _dot(yb_ref[...], wb_ref[...])
    merged = merged + gate_ref[:, 2 * d:3 * d].astype(F32) * _dot(yc_ref[...], wc_ref[...])
    y = _dot(merged.astype(BF16), wo_ref[...])
    out_ref[...] = x_ref[...] + mod_ref[0][2:3] * _rms(y, g_ref[...])


def _mixout(x, mod, g, ya, yb, yc, gates, wa, wb, wc, wo, seq, tm=512):
    t, d = x.shape
    per_b = seq // tm

    def tok(wd):
        return pl.BlockSpec((tm, wd), lambda i: (i, 0))

    return pl.pallas_call(
        _mixout_kernel,
        grid=(t // tm,),
        in_specs=[tok(d), pl.BlockSpec((1, 3, d), lambda i: (i // per_b, 0, 0)), _resident((1, d)),
                  tok(A_V), tok(B_W), tok(C_OUT), tok(GATE_COLS),
                  _resident(wa.shape), _resident(wb.shape), _resident(wc.shape), _resident(wo.shape)],
        out_specs=tok(d),
        out_shape=jax.ShapeDtypeStruct((t, d), F32),
        compiler_params=_cparams(("parallel",)),
        name="mix_out",
    )(x, mod, g, ya, yb, yc, gates, wa, wb, wc, wo)


def kernel(x, c, w_ada, b_ada, norm_g, ffn1_w_in, ffn1_w_out, w_in, hgrn_lb_logits, hgrn_norm_g,
           w_branch_a, w_branch_b, w_branch_c, w_out, ffn2_w_in, ffn2_w_out):
    bsz, seq, d = x.shape
    depth = w_ada.shape[0]
    lb_all = _lower_bounds(hgrn_lb_logits.astype(F32))
    mod = _ada(c, w_ada, b_ada).reshape(depth, bsz, 3, 3, d)
    xt = x.reshape(bsz * seq, d)
    for l in range(depth):
        xt = _ffn_sublayer(xt, mod[l, :, 0], norm_g[l, 0:2], ffn1_w_in[l].astype(BF16),
                           ffn1_w_out[l].astype(BF16), 0.5, seq)
        mix, gates = _inproj(xt, mod[l, :, 1], norm_g[l, 2:3], w_in[l].astype(BF16), seq)
        mix3 = mix.reshape(bsz, seq, MIX_COLS)
        ya = _hgrn(mix3, lb_all[l:l + 1], hgrn_norm_g[l:l + 1]).reshape(bsz * seq, A_V)
        yb = _stick_breaking(mix3).reshape(bsz * seq, B_W)
        yc = _dilated(mix3).reshape(bsz * seq, C_OUT)
        xt = _mixout(xt, mod[l, :, 1], norm_g[l, 3:4], ya, yb, yc, gates,
                     w_branch_a[l].astype(BF16), w_branch_b[l].astype(BF16), w_branch_c[l].astype(BF16),
                     w_out[l].astype(BF16), seq)
        xt = _ffn_sublayer(xt, mod[l, :, 2], norm_g[l, 4:6], ffn2_w_in[l].astype(BF16),
                           ffn2_w_out[l].astype(BF16), 0.5, seq)
    return xt.reshape(bsz, seq, d)
```

```python
import functools
import math

import jax
import jax.numpy as jnp
from jax import lax
from jax.experimental import pallas as pl
from jax.experimental.pallas import tpu as pltpu

D_MODEL = 1024
DEPTH = 4
A_HEADS = 6
A_KDIM = 128
A_VDIM = 64
A_CHUNK = 64
B_HEADS = 6
B_HDIM = 64
B_BLOCK = 128
C_GROUPS = ((128, 1), (512, 4), (2048, 16))
C_HEADS_PER_GROUP = 4
C_HEADS = C_HEADS_PER_GROUP * len(C_GROUPS)
C_HDIM = 64
C_BLOCK = 128
D_FF = 2816
N_BRANCH = 3
EPS = 1e-6
NEG_BIG = -1e30
TINY = 1e-30

A_QK = A_HEADS * A_KDIM
A_V = A_HEADS * A_VDIM
B_W = B_HEADS * B_HDIM
C_W = C_HEADS * C_HDIM
C_OUT = C_HEADS_PER_GROUP * C_HDIM
MIX_COLS = 2 * A_QK + 2 * A_V + 3 * B_W + 3 * C_W
GATE_COLS = N_BRANCH * D_MODEL
IN_COLS = MIX_COLS + GATE_COLS

LANES = 128
MXU_WIDTH = 256
OFF_AQ = 0
OFF_AF = A_QK // LANES
OFF_AI = 2 * A_QK // LANES
OFF_AG = (2 * A_QK + A_V) // LANES
OFF_BQ = (2 * A_QK + 2 * A_V) // LANES
OFF_BK = OFF_BQ + B_W // LANES
OFF_BV = OFF_BK + B_W // LANES
OFF_CQ = OFF_BV + B_W // LANES
OFF_CK = OFF_CQ + C_W // LANES
OFF_CV = OFF_CK + C_W // LANES
MIX_BLOCKS = MIX_COLS // LANES

VMEM_LIMIT = 56 * 1024 * 1024
SB_SKIP_LOG = -105.0
SB_FIXED_BLOCKS = 3
SB_QBLOCKS_PER_ITER = 2
DIL_BATCH = 8
HGRN_SAFE_SPAN = 80.0

F32 = jnp.float32
BF16 = jnp.bfloat16


def _alibi_slopes(n):
    def pow2_slopes(m):
        start = 2.0 ** (-8.0 / m)
        return [start ** (i + 1) for i in range(m)]
    if math.log2(n).is_integer():
        s = pow2_slopes(n)
    else:
        c = 2 ** int(math.floor(math.log2(n)))
        s = pow2_slopes(c) + pow2_slopes(2 * c)[0::2][: n - c]
    return sorted(s, reverse=True)


def _rms(x, g):
    return x * lax.rsqrt(jnp.mean(x * x, axis=-1, keepdims=True) + EPS) * g


def _sigmoid(x):
    return 1.0 / (1.0 + jnp.exp(-x))


def _dot(a, b):
    return jnp.dot(a, b, preferred_element_type=F32)


def _dot_nt(a, b):
    return lax.dot_general(a, b, (((1,), (1,)), ((), ())), preferred_element_type=F32)


def _split_bf16(x):
    hi = x.astype(BF16)
    lo = (x - hi.astype(F32)).astype(BF16)
    return hi, lo


def _cparams(sem):
    return pltpu.CompilerParams(dimension_semantics=sem, vmem_limit_bytes=VMEM_LIMIT)


def _resident(shape):
    nd = len(shape)
    return pl.BlockSpec(shape, lambda *_: (0,) * nd, pipeline_mode=pl.Buffered(1))


def _ada_kernel(c_ref, w_ref, b_ref, o_ref):
    c = c_ref[...]
    ca = c * _sigmoid(c)
    o_ref[0] = jnp.dot(ca, w_ref[0], precision=lax.Precision.HIGHEST,
                       preferred_element_type=F32) + b_ref[0]


def _ada(c, w_ada, b_ada):
    depth, d, n = w_ada.shape
    bsz = c.shape[0]
    tn = 1152
    return pl.pallas_call(
        _ada_kernel,
        grid=(depth, n // tn),
        in_specs=[pl.BlockSpec((bsz, d), lambda l, j: (0, 0)),
                  pl.BlockSpec((1, d, tn), lambda l, j: (l, 0, j)),
                  pl.BlockSpec((1, 1, tn), lambda l, j: (l, 0, j))],
        out_specs=pl.BlockSpec((1, bsz, tn), lambda l, j: (l, 0, j)),
        out_shape=jax.ShapeDtypeStruct((depth, bsz, n), F32),
        compiler_params=_cparams(("parallel", "parallel")),
        name="ada_mod",
    )(c, w_ada, b_ada.reshape(depth, 1, n))


def _lb_kernel(l_ref, o_ref):
    x = l_ref[...]
    e = jnp.exp(x - jnp.max(x, axis=0, keepdims=True))
    p = e / jnp.sum(e, axis=0, keepdims=True)
    depth = x.shape[0]
    run = p[0:1]
    o_ref[0:1, :] = run - p[0:1]
    for l in range(1, depth):
        run = run + p[l:l + 1]
        o_ref[l:l + 1, :] = run - p[0:1]


def _lower_bounds(logits):
    return pl.pallas_call(_lb_kernel, out_shape=jax.ShapeDtypeStruct(logits.shape, F32),
                          name="hgrn_lower_bounds")(logits)


def _ffn_kernel(x_ref, mod_ref, g_ref, win_ref, wout_ref, o_ref, act_ref, *, res_w, ffc):
    x = x_ref[...]
    m = mod_ref[0]
    h = _rms(x, g_ref[0:1]) * (1.0 + m[1:2]) + m[0:1]
    hb = h.astype(BF16)
    d_ff = wout_ref.shape[0]
    for c in range(d_ff // ffc):
        a = _dot(hb, win_ref[:, c * ffc:(c + 1) * ffc])
        b = _dot(hb, win_ref[:, d_ff + c * ffc:d_ff + (c + 1) * ffc])
        act_ref[:, c * ffc:(c + 1) * ffc] = (a * _sigmoid(a) * b).astype(BF16)
    y = _dot(act_ref[...], wout_ref[...])
    o_ref[...] = x + res_w * m[2:3] * _rms(y, g_ref[1:2])


def _ffn_sublayer(x, mod, g2, w_in, w_out, res_w, seq, tm=512, ffc=256):
    t, d = x.shape
    d_ff = w_out.shape[0]
    per_b = seq // tm
    return pl.pallas_call(
        functools.partial(_ffn_kernel, res_w=res_w, ffc=ffc),
        grid=(t // tm,),
        in_specs=[pl.BlockSpec((tm, d), lambda i: (i, 0)),
                  pl.BlockSpec((1, 3, d), lambda i: (i // per_b, 0, 0)),
                  _resident((2, d)),
                  _resident((d, 2 * d_ff)),
                  _resident((d_ff, d))],
        out_specs=pl.BlockSpec((tm, d), lambda i: (i, 0)),
        out_shape=jax.ShapeDtypeStruct((t, d), F32),
        scratch_shapes=[pltpu.VMEM((tm, d_ff), BF16)],
        compiler_params=_cparams(("parallel",)),
        name="ffn_sublayer",
    )(x, mod, g2, w_in, w_out)


def _inproj_kernel(x_ref, mod_ref, g_ref, w_ref, mix_ref, gate_ref, *, cc):
    x = x_ref[...]
    m = mod_ref[0]
    h = _rms(x, g_ref[...]) * (1.0 + m[1:2]) + m[0:1]
    hb = h.astype(BF16)
    for lo in range(0, IN_COLS, cc):
        hi = min(lo + cc, IN_COLS)
        z = _dot(hb, w_ref[:, lo:hi])
        mix_hi = min(hi, MIX_COLS)
        if lo < mix_hi:
            mix_ref[:, lo:mix_hi] = z[:, :mix_hi - lo]
        if hi > MIX_COLS:
            g_lo = max(lo, MIX_COLS)
            gate_ref[:, g_lo - MIX_COLS:hi - MIX_COLS] = _sigmoid(z[:, g_lo - lo:]).astype(BF16)


def _inproj(x, mod, g, w_in, seq, tm=512, cc=MXU_WIDTH):
    t, d = x.shape
    per_b = seq // tm
    return pl.pallas_call(
        functools.partial(_inproj_kernel, cc=cc),
        grid=(t // tm,),
        in_specs=[pl.BlockSpec((tm, d), lambda i: (i, 0)),
                  pl.BlockSpec((1, 3, d), lambda i: (i // per_b, 0, 0)),
                  _resident((1, d)),
                  _resident((d, IN_COLS))],
        out_specs=[pl.BlockSpec((tm, MIX_COLS), lambda i: (i, 0)),
                   pl.BlockSpec((tm, GATE_COLS), lambda i: (i, 0))],
        out_shape=[jax.ShapeDtypeStruct((t, MIX_COLS), F32),
                   jax.ShapeDtypeStruct((t, GATE_COLS), BF16)],
        compiler_params=_cparams(("parallel",)),
        name="mix_inproj",
    )(x, mod, g, w_in)


def _hgrn_kernel(q_ref, f_ref, i_ref, g_ref, lb_ref, ng_ref, o_ref,
                 state_ref, qs_ref, ks_ref, bs_ref, oi_ref, *, ts):
    ck, kd, vd = A_CHUNK, A_KDIM, A_VDIM
    nc = ts // ck
    mid = ck // 2 - 1

    @pl.when(pl.program_id(2) == 0)
    def _():
        state_ref[...] = jnp.zeros_like(state_ref)

    row = lax.broadcasted_iota(jnp.int32, (ck, ck), 0)
    col = lax.broadcasted_iota(jnp.int32, (ck, ck), 1)
    tril = row >= col
    tri_incl = tril.astype(BF16)
    row1 = lax.broadcasted_iota(jnp.int32, (ck, 1), 0)
    ng = ng_ref[...]

    def finish(o, rows, j):
        gate = g_ref[0, rows, j * vd:(j + 1) * vd]
        return _rms(o, ng) * (gate * _sigmoid(gate))

    heads = []
    worst = None
    for j in range(2):
        ksl = slice(j * kd, (j + 1) * kd)
        qraw = q_ref[0, :, ksl]
        fr = f_ref[0, :, ksl]
        lb = lb_ref[:, ksl]
        qs = qraw * _sigmoid(qraw)
        e = jnp.exp(-jnp.abs(fr))
        big = 1.0 / (1.0 + e)
        small = e * big
        pos = fr >= 0.0
        f = lb + (1.0 - lb) * jnp.where(pos, big, small)
        k = (1.0 - lb) * jnp.where(pos, small, big)
        lf_hi, lf_lo = _split_bf16(jnp.log(jnp.maximum(f, TINY)))
        b = jnp.concatenate(
            [_dot(tri_incl, lf_hi[c * ck:(c + 1) * ck]) + _dot(tri_incl, lf_lo[c * ck:(c + 1) * ck])
             for c in range(nc)], axis=0)
        heads.append((qs, k, b))
        for c in range(nc):
            b_mid = b[c * ck + mid:c * ck + mid + 1]
            b_end = b[(c + 1) * ck - 1:(c + 1) * ck]
            w = jnp.minimum(b_mid, b_end - b_mid)
            worst = w if worst is None else jnp.minimum(worst, w)
    safe = jnp.min(worst) > -HGRN_SAFE_SPAN

    @pl.when(safe)
    def _():
        units = [(c, j) for c in range(nc) for j in range(2)]
        rows_of = lambda c: slice(c * ck, (c + 1) * ck)
        qa, ka, q_dec, k_end, end_decay, vb = {}, {}, {}, {}, {}, {}
        for c, j in units:
            qs, k, b = heads[j]
            bc = b[rows_of(c)]
            b_mid = bc[mid:mid + 1]
            b_end = bc[ck - 1:ck]
            qa_f = qs[rows_of(c)] * jnp.exp(bc - b_mid)
            ka_f = k[rows_of(c)] * jnp.exp(b_mid - bc)
            qa[c, j], ka[c, j] = qa_f.astype(BF16), ka_f.astype(BF16)
            q_dec[c, j] = (qa_f * jnp.exp(b_mid)).astype(BF16)
            k_end[c, j] = (ka_f * jnp.exp(b_end - b_mid)).astype(BF16)
            end_decay[c, j] = jnp.exp(b_end)
            vb[c, j] = i_ref[0, rows_of(c), j * vd:(j + 1) * vd].astype(BF16)
        scores = {u: jnp.where(tril, _dot_nt(qa[u], ka[u]), 0.0).astype(BF16) for u in units}
        o_intra = {u: _dot(scores[u], vb[u]) for u in units}
        update = {u: lax.dot_general(vb[u], k_end[u], (((0,), (0,)), ((), ())), preferred_element_type=F32)
                  for u in units}
        state_before = {}
        for j in range(2):
            st = state_ref[j]
            for c in range(nc):
                state_before[c, j] = st.astype(BF16)
                st = end_decay[c, j] * st + update[c, j]
            state_ref[j] = st
        for c in range(nc):
            outs = [finish(o_intra[c, j] + _dot_nt(q_dec[c, j], state_before[c, j]), rows_of(c), j)
                    for j in range(2)]
            o_ref[0, rows_of(c), :] = jnp.concatenate(outs, axis=1).astype(o_ref.dtype)

    @pl.when(jnp.logical_not(safe))
    def _():
        for j in range(2):
            qs, k, b = heads[j]
            qs_ref[...] = qs
            ks_ref[...] = k
            bs_ref[...] = b
            vsl = slice(j * vd, (j + 1) * vd)

            def chunk(c, carry):
                r0 = pl.multiple_of(c * ck, ck)
                rows = pl.ds(r0, ck)
                bc = bs_ref[rows, :]
                qc = qs_ref[rows, :]
                kc = ks_ref[rows, :]
                v = i_ref[0, rows, vsl]
                vb = v.astype(BF16)

                def one_row(t, cr):
                    bt = bs_ref[pl.ds(r0 + t, 1), :]
                    qt = qs_ref[pl.ds(r0 + t, 1), :]
                    dec = jnp.exp(jnp.where(row1 <= t, bt - bc, NEG_BIG))
                    sc = jnp.sum(dec * qt * kc, axis=-1, keepdims=True)
                    oi_ref[pl.ds(t, 1), :] = jnp.sum(sc * v, axis=0, keepdims=True)
                    return cr

                lax.fori_loop(0, ck, one_row, 0)
                b_end = bc[ck - 1:ck]
                st = state_ref[j]
                o = oi_ref[...] + _dot_nt((qc * jnp.exp(bc)).astype(BF16), st.astype(BF16))
                k_end = (kc * jnp.exp(b_end - bc)).astype(BF16)
                state_ref[j] = jnp.exp(b_end) * st + lax.dot_general(
                    vb, k_end, (((0,), (0,)), ((), ())), preferred_element_type=F32)
                o_ref[0, rows, vsl] = finish(o, rows, j).astype(o_ref.dtype)
                return carry

            lax.fori_loop(0, nc, chunk, 0)


def _hgrn(mix3, lb, ng, ts=1024):
    bsz, seq, _ = mix3.shape
    pairs = A_HEADS // 2
    kw, vw = 2 * A_KDIM, 2 * A_VDIM
    return pl.pallas_call(
        functools.partial(_hgrn_kernel, ts=ts),
        grid=(bsz, pairs, seq // ts),
        in_specs=[pl.BlockSpec((1, ts, kw), lambda b, p, s: (b, s, OFF_AQ * LANES // kw + p)),
                  pl.BlockSpec((1, ts, kw), lambda b, p, s: (b, s, OFF_AF * LANES // kw + p)),
                  pl.BlockSpec((1, ts, vw), lambda b, p, s: (b, s, OFF_AI + p)),
                  pl.BlockSpec((1, ts, vw), lambda b, p, s: (b, s, OFF_AG + p)),
                  pl.BlockSpec((1, kw), lambda b, p, s: (0, p)),
                  pl.BlockSpec((1, A_VDIM), lambda b, p, s: (0, 0))],
        out_specs=pl.BlockSpec((1, ts, vw), lambda b, p, s: (b, s, p)),
        out_shape=jax.ShapeDtypeStruct((bsz, seq, A_V), BF16),
        scratch_shapes=[pltpu.VMEM((2, A_VDIM, A_KDIM), F32),
                        pltpu.VMEM((ts, A_KDIM), F32),
                        pltpu.VMEM((ts, A_KDIM), F32),
                        pltpu.VMEM((ts, A_KDIM), F32),
                        pltpu.VMEM((A_CHUNK, A_VDIM), F32)],
        compiler_params=_cparams(("parallel", "parallel", "arbitrary")),
        name="hgrn2",
    )(mix3, mix3, mix3, mix3, lb, ng)


def _sb_kernel(q_ref, k_ref, v_ref, o_ref, run_ref, acc_ref):
    blk, d = B_BLOCK, B_HDIM
    seq = q_ref.shape[1]
    neg_scale = -(d ** -0.5)
    row = lax.broadcasted_iota(jnp.int32, (blk, blk), 0)
    col = lax.broadcasted_iota(jnp.int32, (blk, blk), 1)
    strictly_before = col < row
    r2 = lax.broadcasted_iota(jnp.int32, (blk, 2 * blk), 0)
    c2 = lax.broadcasted_iota(jnp.int32, (blk, 2 * blk), 1)
    later_and_ones = jnp.logical_or(r2 > c2, c2 >= blk).astype(BF16)

    def load_neg_q(q0, j):
        return (q_ref[0, pl.ds(q0, blk), j * d:(j + 1) * d] * neg_scale).astype(BF16)

    def logs(neg_z):
        log_1m = jnp.minimum(neg_z, 0.0) - jnp.log(1.0 + jnp.exp(-jnp.abs(neg_z)))
        return log_1m, log_1m - neg_z

    def later_sums(hi, lo):
        return _dot(hi, later_and_ones) + _dot(lo, later_and_ones)

    def qgroup(qg, carry0):
        qbs = [qg * SB_QBLOCKS_PER_ITER + i for i in range(SB_QBLOCKS_PER_ITER)]
        q0s = [pl.multiple_of(qb * blk, blk) for qb in qbs]
        units = [(i, j, step) for i in range(SB_QBLOCKS_PER_ITER) for j in range(2)
                 for step in range(SB_FIXED_BLOCKS)]
        keeps, vals, neg_z = {}, {}, {}
        for i, j, step in units:
            hs = slice(j * d, (j + 1) * d)
            kb = qbs[i] - step
            k0 = pl.multiple_of(jnp.maximum(kb, 0) * blk, blk)
            keeps[i, j, step] = strictly_before if step == 0 else kb >= 0
            vals[i, j, step] = v_ref[0, pl.ds(k0, blk), hs].astype(BF16)
            neg_z[i, j, step] = _dot_nt(load_neg_q(q0s[i], j), k_ref[0, pl.ds(k0, blk), hs].astype(BF16))
        log_bs, sums = {}, {}
        for u in units:
            log_1m, log_bs[u] = logs(neg_z[u])
            sums[u] = _split_bf16(jnp.where(keeps[u], log_1m, 0.0))
        for u in units:
            sums[u] = later_sums(*sums[u])
        weights = {}
        tops = {}
        for i in range(SB_QBLOCKS_PER_ITER):
            for j in range(2):
                run = jnp.zeros((blk, blk), F32)
                for step in range(SB_FIXED_BLOCKS):
                    u = (i, j, step)
                    within, total = sums[u][:, :blk], sums[u][:, blk:]
                    weights[u] = jnp.where(keeps[u], jnp.exp(log_bs[u] + within + run), 0.0).astype(BF16)
                    run = run + total
                run_ref[i, j] = run
                tops[i, j] = jnp.max(run)
        for i in range(SB_QBLOCKS_PER_ITER):
            for j in range(2):
                acc = _dot(weights[i, j, 0], vals[i, j, 0])
                for step in range(1, SB_FIXED_BLOCKS):
                    acc = acc + _dot(weights[i, j, step], vals[i, j, step])
                acc_ref[i, j] = acc

        def cond(st):
            kb, top = st
            return jnp.logical_and(kb >= 0, top > SB_SKIP_LOG)

        for i in range(SB_QBLOCKS_PER_ITER):
            def body(st, i=i):
                kb, _ = st
                k0 = pl.multiple_of(kb * blk, blk)
                new_tops = []
                for j in range(2):
                    hs = slice(j * d, (j + 1) * d)
                    v = v_ref[0, pl.ds(k0, blk), hs].astype(BF16)
                    log_1m, log_b = logs(_dot_nt(load_neg_q(q0s[i], j), k_ref[0, pl.ds(k0, blk), hs].astype(BF16)))
                    r = later_sums(*_split_bf16(log_1m))
                    run = run_ref[i, j]
                    a = jnp.exp(log_b + r[:, :blk] + run)
                    acc_ref[i, j] = acc_ref[i, j] + _dot(a.astype(BF16), v)
                    run = run + r[:, blk:]
                    run_ref[i, j] = run
                    new_tops.append(jnp.max(run))
                return kb - 1, jnp.maximum(new_tops[0], new_tops[1])

            lax.while_loop(cond, body, (qbs[i] - SB_FIXED_BLOCKS, jnp.maximum(tops[i, 0], tops[i, 1])))
            o_ref[0, pl.ds(q0s[i], blk), :] = jnp.concatenate(
                [acc_ref[i, 0], acc_ref[i, 1]], axis=1).astype(o_ref.dtype)
        return carry0

    lax.fori_loop(0, seq // (blk * SB_QBLOCKS_PER_ITER), qgroup, 0)


def _stick_breaking(mix3):
    bsz, seq, _ = mix3.shape
    pairs = B_HEADS // 2
    w = 2 * B_HDIM
    return pl.pallas_call(
        _sb_kernel,
        grid=(bsz, pairs),
        in_specs=[pl.BlockSpec((1, seq, w), lambda b, p: (b, 0, OFF_BQ + p)),
                  pl.BlockSpec((1, seq, w), lambda b, p: (b, 0, OFF_BK + p)),
                  pl.BlockSpec((1, seq, w), lambda b, p: (b, 0, OFF_BV + p))],
        out_specs=pl.BlockSpec((1, seq, w), lambda b, p: (b, 0, p)),
        out_shape=jax.ShapeDtypeStruct((bsz, seq, B_W), BF16),
        scratch_shapes=[pltpu.VMEM((SB_QBLOCKS_PER_ITER, 2, B_BLOCK, B_BLOCK), F32),
                        pltpu.VMEM((SB_QBLOCKS_PER_ITER, 2, B_BLOCK, B_HDIM), F32)],
        compiler_params=_cparams(("parallel", "parallel")),
        name="stick_breaking",
    )(mix3, mix3, mix3)


def _dil_kernel(*refs, tile, slopes):
    blk, d = C_BLOCK, C_HDIM
    n_groups = len(C_GROUPS)
    o_ref, os_ref, ls_ref = refs[5 * n_groups:]
    pair = pl.program_id(1)
    first_tile = pl.program_id(2) == 0
    scale = d ** -0.5
    qi = lax.broadcasted_iota(jnp.int32, (blk, 2 * blk), 0)
    kj = lax.broadcasted_iota(jnp.int32, (blk, 2 * blk), 1)
    delta = qi + blk - kj
    for g, (window, dil) in enumerate(C_GROUPS):
        q_ref, k_ref, kp_ref, v_ref, vp_ref = refs[5 * g:5 * g + 5]
        window_ok = jnp.logical_and(delta >= 0, delta <= window // dil)
        window_ok_first = jnp.logical_and(window_ok, jnp.logical_or(kj >= blk, jnp.logical_not(first_tile)))
        dist = (delta * dil).astype(F32)
        alibi = [jnp.where(pair == 0, jnp.float32(slopes[g][j]), jnp.float32(slopes[g][2 + j])) * dist
                 for j in range(2)]
        def strided(ref, rho, block):
            start = rho + block * blk * dil
            return ref[0, pl.ds(start, blk, stride=dil), :] if dil > 1 else ref[0, pl.ds(start, blk), :]

        blocks = [(rho, i) for rho in range(dil) for i in range(tile // (dil * blk))]
        for at in range(0, len(blocks), DIL_BATCH):
            batch = blocks[at:at + DIL_BATCH]
            units = [(rho, i, j) for rho, i in batch for j in range(2)]
            scores, values = {}, {}
            for rho, i in batch:
                q_both = strided(q_ref, rho, i)
                k_both = jnp.concatenate([strided(kp_ref, rho, 0) if i == 0 else strided(k_ref, rho, i - 1),
                                          strided(k_ref, rho, i)], axis=0)
                v_both = jnp.concatenate([strided(vp_ref, rho, 0) if i == 0 else strided(v_ref, rho, i - 1),
                                          strided(v_ref, rho, i)], axis=0)
                for j in range(2):
                    hs = slice(j * d, (j + 1) * d)
                    scores[rho, i, j] = _dot_nt((q_both[:, hs] * scale).astype(BF16), k_both[:, hs].astype(BF16))
                    values[rho, i, j] = v_both[:, hs].astype(BF16)
            probs, dens, lses = {}, {}, {}
            for rho, i, j in units:
                valid = window_ok_first if i == 0 else window_ok
                s = jnp.where(valid, scores[rho, i, j] - alibi[j], NEG_BIG)
                mx = jnp.max(s, axis=-1, keepdims=True)
                p = jnp.exp(s - mx)
                dens[rho, i, j] = jnp.sum(p, axis=-1, keepdims=True)
                probs[rho, i, j] = p.astype(BF16)
                lses[rho, i, j] = jnp.broadcast_to(mx + jnp.log(dens[rho, i, j]), (blk, d))
            outs = {u: _dot(probs[u], values[u]) / dens[u] for u in units}
            for rho, i in batch:
                start = rho + i * blk * dil
                rows = pl.ds(start, blk, stride=dil) if dil > 1 else pl.ds(start, blk)
                os_ref[g, rows, :] = jnp.concatenate([outs[rho, i, 0], outs[rho, i, 1]], axis=1)
                ls_ref[g, rows, :] = jnp.concatenate([lses[rho, i, 0], lses[rho, i, 1]], axis=1)
    lses = [ls_ref[g] for g in range(n_groups)]
    top = functools.reduce(jnp.maximum, lses)
    wts = [jnp.exp(l - top) for l in lses]
    num = functools.reduce(lambda a, b: a + b, [wts[g] * os_ref[g] for g in range(n_groups)])
    o_ref[0] = (num / functools.reduce(lambda a, b: a + b, wts)).astype(o_ref.dtype)


def _dilated(mix3):
    bsz, seq, _ = mix3.shape
    pairs = C_HEADS_PER_GROUP // 2
    w = 2 * C_HDIM
    tile = C_BLOCK * max(dil for _, dil in C_GROUPS)
    all_slopes = _alibi_slopes(C_HEADS)
    slopes = tuple(tuple(all_slopes[g * C_HEADS_PER_GROUP:(g + 1) * C_HEADS_PER_GROUP])
                   for g in range(len(C_GROUPS)))
    in_specs = []
    for g, (_, dil) in enumerate(C_GROUPS):
        back = C_BLOCK * dil
        per = tile // back

        def cur(off, g=g):
            return pl.BlockSpec((1, tile, w), lambda b, p, n: (b, n, off + pairs * g + p))

        def prev(off, g=g, back=back, per=per):
            return pl.BlockSpec((1, back, w), lambda b, p, n: (b, jnp.maximum(n * per - 1, 0), off + pairs * g + p))

        in_specs += [cur(OFF_CQ), cur(OFF_CK), prev(OFF_CK), cur(OFF_CV), prev(OFF_CV)]
    return pl.pallas_call(
        functools.partial(_dil_kernel, tile=tile, slopes=slopes),
        grid=(bsz, pairs, seq // tile),
        in_specs=in_specs,
        out_specs=pl.BlockSpec((1, tile, w), lambda b, p, n: (b, n, p)),
        out_shape=jax.ShapeDtypeStruct((bsz, seq, C_OUT), BF16),
        scratch_shapes=[pltpu.VMEM((len(C_GROUPS), tile, w), F32),
                        pltpu.VMEM((len(C_GROUPS), tile, w), F32)],
        compiler_params=_cparams(("parallel", "parallel", "arbitrary")),
        name="dilated",
    )(*([mix3] * (5 * len(C_GROUPS))))


def _mixout_kernel(x_ref, mod_ref, g_ref, ya_ref, yb_ref, yc_ref, gate_ref, wa_ref, wb_ref, wc_ref, wo_ref,
                   out_ref):
    d = D_MODEL
    merged = gate_ref[:, 0:d].astype(F32) * _dot(ya_ref[...], wa_ref[...])
    merged = merged + gate_ref[:, d:2 * d].astype(F32) * _dot(yb_ref[...], wb_ref[...])
    merged = merged + gate_ref[:, 2 * d:3 * d].astype(F32) * _dot(yc_ref[...], wc_ref[...])
    y = _dot(merged.astype(BF16), wo_ref[...])
    out_ref[...] = x_ref[...] + mod_ref[0][2:3] * _rms(y, g_ref[...])


def _mixout(x, mod, g, ya, yb, yc, gates, wa, wb, wc, wo, seq, tm=512):
    t, d = x.shape
    per_b = seq // tm

    def tok(wd):
        return pl.BlockSpec((tm, wd), lambda i: (i, 0))

    return pl.pallas_call(
        _mixout_kernel,
        grid=(t // tm,),
        in_specs=[tok(d), pl.BlockSpec((1, 3, d), lambda i: (i // per_b, 0, 0)), _resident((1, d)),
                  tok(A_V), tok(B_W), tok(C_OUT), tok(GATE_COLS),
                  _resident(wa.shape), _resident(wb.shape), _resident(wc.shape), _resident(wo.shape)],
        out_specs=tok(d),
        out_shape=jax.ShapeDtypeStruct((t, d), F32),
        compiler_params=_cparams(("parallel",)),
        name="mix_out",
    )(x, mod, g, ya, yb, yc, gates, wa, wb, wc, wo)


def kernel(x, c, w_ada, b_ada, norm_g, ffn1_w_in, ffn1_w_out, w_in, hgrn_lb_logits, hgrn_norm_g,
           w_branch_a, w_branch_b, w_branch_c, w_out, ffn2_w_in, ffn2_w_out):
    bsz, seq, d = x.shape
    depth = w_ada.shape[0]
    lb_all = _lower_bounds(hgrn_lb_logits.astype(F32))
    mod = _ada(c, w_ada, b_ada).reshape(depth, bsz, 3, 3, d)
    xt = x.reshape(bsz * seq, d)
    for l in range(depth):
        xt = _ffn_sublayer(xt, mod[l, :, 0], norm_g[l, 0:2], ffn1_w_in[l].astype(BF16),
                           ffn1_w_out[l].astype(BF16), 0.5, seq)
        mix, gates = _inproj(xt, mod[l, :, 1], norm_g[l, 2:3], w_in[l].astype(BF16), seq)
        mix3 = mix.reshape(bsz, seq, MIX_COLS)
        ya = _hgrn(mix3, lb_all[l:l + 1], hgrn_norm_g[l:l + 1]).reshape(bsz * seq, A_V)
        yb = _stick_breaking(mix3).reshape(bsz * seq, B_W)
        yc = _dilated(mix3).reshape(bsz * seq, C_OUT)
        xt = _mixout(xt, mod[l, :, 1], norm_g[l, 3:4], ya, yb, yc, gates,
                     w_branch_a[l].astype(BF16), w_branch_b[l].astype(BF16), w_branch_c[l].astype(BF16),
                     w_out[l].astype(BF16), seq)
        xt = _ffn_sublayer(xt, mod[l, :, 2], norm_g[l, 4:6], ffn2_w_in[l].astype(BF16),
                           ffn2_w_out[l].astype(BF16), 0.5, seq)
    return xt.reshape(bsz, seq, d)
```

```python
import functools
import math

import jax
import jax.numpy as jnp
from jax import lax
from jax.experimental import pallas as pl
from jax.experimental.pallas import tpu as pltpu

D_MODEL = 1024
DEPTH = 4
A_HEADS = 6
A_KDIM = 128
A_VDIM = 64
A_CHUNK = 64
B_HEADS = 6
B_HDIM = 64
B_BLOCK = 128
C_GROUPS = ((128, 1), (512, 4), (2048, 16))
C_HEADS_PER_GROUP = 4
C_HEADS = C_HEADS_PER_GROUP * len(C_GROUPS)
C_HDIM = 64
C_BLOCK = 128
D_FF = 2816
N_BRANCH = 3
EPS = 1e-6
NEG_BIG = -1e30
TINY = 1e-30

A_QK = A_HEADS * A_KDIM
A_V = A_HEADS * A_VDIM
B_W = B_HEADS * B_HDIM
C_W = C_HEADS * C_HDIM
C_OUT = C_HEADS_PER_GROUP * C_HDIM
MIX_COLS = 2 * A_QK + 2 * A_V + 3 * B_W + 3 * C_W
GATE_COLS = N_BRANCH * D_MODEL
IN_COLS = MIX_COLS + GATE_COLS

LANES = 128
MXU_WIDTH = 256
OFF_AQ = 0
OFF_AF = A_QK // LANES
OFF_AI = 2 * A_QK // LANES
OFF_AG = (2 * A_QK + A_V) // LANES
OFF_BQ = (2 * A_QK + 2 * A_V) // LANES
OFF_BK = OFF_BQ + B_W // LANES
OFF_BV = OFF_BK + B_W // LANES
OFF_CQ = OFF_BV + B_W // LANES
OFF_CK = OFF_CQ + C_W // LANES
OFF_CV = OFF_CK + C_W // LANES
MIX_BLOCKS = MIX_COLS // LANES

VMEM_LIMIT = 56 * 1024 * 1024
SB_SKIP_LOG = -105.0
SB_FIXED_BLOCKS = 3
SB_QBLOCKS_PER_ITER = 2
DIL_BATCH = 8
HGRN_SAFE_SPAN = 80.0

F32 = jnp.float32
BF16 = jnp.bfloat16


def _alibi_slopes(n):
    def pow2_slopes(m):
        start = 2.0 ** (-8.0 / m)
        return [start ** (i + 1) for i in range(m)]
    if math.log2(n).is_integer():
        s = pow2_slopes(n)
    else:
        c = 2 ** int(math.floor(math.log2(n)))
        s = pow2_slopes(c) + pow2_slopes(2 * c)[0::2][: n - c]
    return sorted(s, reverse=True)


def _rms(x, g):
    return x * lax.rsqrt(jnp.mean(x * x, axis=-1, keepdims=True) + EPS) * g


def _sigmoid(x):
    return 1.0 / (1.0 + jnp.exp(-x))


def _dot(a, b):
    return jnp.dot(a, b, preferred_element_type=F32)


def _dot_nt(a, b):
    return lax.dot_general(a, b, (((1,), (1,)), ((), ())), preferred_element_type=F32)


def _split_bf16(x):
    hi = x.astype(BF16)
    lo = (x - hi.astype(F32)).astype(BF16)
    return hi, lo


def _cparams(sem):
    return pltpu.CompilerParams(dimension_semantics=sem, vmem_limit_bytes=VMEM_LIMIT)


def _resident(shape):
    nd = len(shape)
    return pl.BlockSpec(shape, lambda *_: (0,) * nd, pipeline_mode=pl.Buffered(1))


def _ada_kernel(c_ref, w_ref, b_ref, o_ref):
    c = c_ref[...]
    ca = c * _sigmoid(c)
    o_ref[0] = jnp.dot(ca, w_ref[0], precision=lax.Precision.HIGHEST,
                       preferred_element_type=F32) + b_ref[0]


def _ada(c, w_ada, b_ada):
    depth, d, n = w_ada.shape
    bsz = c.shape[0]
    tn = 1152
    return pl.pallas_call(
        _ada_kernel,
        grid=(depth, n // tn),
        in_specs=[pl.BlockSpec((bsz, d), lambda l, j: (0, 0)),
                  pl.BlockSpec((1, d, tn), lambda l, j: (l, 0, j)),
                  pl.BlockSpec((1, 1, tn), lambda l, j: (l, 0, j))],
        out_specs=pl.BlockSpec((1, bsz, tn), lambda l, j: (l, 0, j)),
        out_shape=jax.ShapeDtypeStruct((depth, bsz, n), F32),
        compiler_params=_cparams(("parallel", "parallel")),
        name="ada_mod",
    )(c, w_ada, b_ada.reshape(depth, 1, n))


def _lb_kernel(l_ref, o_ref):
    x = l_ref[...]
    e = jnp.exp(x - jnp.max(x, axis=0, keepdims=True))
    p = e / jnp.sum(e, axis=0, keepdims=True)
    depth = x.shape[0]
    run = p[0:1]
    o_ref[0:1, :] = run - p[0:1]
    for l in range(1, depth):
        run = run + p[l:l + 1]
        o_ref[l:l + 1, :] = run - p[0:1]


def _lower_bounds(logits):
    return pl.pallas_call(_lb_kernel, out_shape=jax.ShapeDtypeStruct(logits.shape, F32),
                          name="hgrn_lower_bounds")(logits)


def _ffn_kernel(x_ref, mod_ref, g_ref, win_ref, wout_ref, o_ref, act_ref, *, res_w, ffc):
    x = x_ref[...]
    m = mod_ref[0]
    h = _rms(x, g_ref[0:1]) * (1.0 + m[1:2]) + m[0:1]
    hb = h.astype(BF16)
    d_ff = wout_ref.shape[0]
    for c in range(d_ff // ffc):
        a = _dot(hb, win_ref[:, c * ffc:(c + 1) * ffc])
        b = _dot(hb, win_ref[:, d_ff + c * ffc:d_ff + (c + 1) * ffc])
        act_ref[:, c * ffc:(c + 1) * ffc] = (a * _sigmoid(a) * b).astype(BF16)
    y = _dot(act_ref[...], wout_ref[...])
    o_ref[...] = x + res_w * m[2:3] * _rms(y, g_ref[1:2])


def _ffn_sublayer(x, mod, g2, w_in, w_out, res_w, seq, tm=512, ffc=256):
    t, d = x.shape
    d_ff = w_out.shape[0]
    per_b = seq // tm
    return pl.pallas_call(
        functools.partial(_ffn_kernel, res_w=res_w, ffc=ffc),
        grid=(t // tm,),
        in_specs=[pl.BlockSpec((tm, d), lambda i: (i, 0)),
                  pl.BlockSpec((1, 3, d), lambda i: (i // per_b, 0, 0)),
                  _resident((2, d)),
                  _resident((d, 2 * d_ff)),
                  _resident((d_ff, d))],
        out_specs=pl.BlockSpec((tm, d), lambda i: (i, 0)),
        out_shape=jax.ShapeDtypeStruct((t, d), F32),
        scratch_shapes=[pltpu.VMEM((tm, d_ff), BF16)],
        compiler_params=_cparams(("parallel",)),
        name="ffn_sublayer",
    )(x, mod, g2, w_in, w_out)


def _inproj_kernel(x_ref, mod_ref, g_ref, w_ref, mix_ref, gate_ref, *, cc):
    x = x_ref[...]
    m = mod_ref[0]
    h = _rms(x, g_ref[...]) * (1.0 + m[1:2]) + m[0:1]
    hb = h.astype(BF16)
    for lo in range(0, IN_COLS, cc):
        hi = min(lo + cc, IN_COLS)
        z = _dot(hb, w_ref[:, lo:hi])
        mix_hi = min(hi, MIX_COLS)
        if lo < mix_hi:
            mix_ref[:, lo:mix_hi] = z[:, :mix_hi - lo]
        if hi > MIX_COLS:
            g_lo = max(lo, MIX_COLS)
            gate_ref[:, g_lo - MIX_COLS:hi - MIX_COLS] = _sigmoid(z[:, g_lo - lo:]).astype(BF16)


def _inproj(x, mod, g, w_in, seq, tm=512, cc=MXU_WIDTH):
    t, d = x.shape
    per_b = seq // tm
    return pl.pallas_call(
        functools.partial(_inproj_kernel, cc=cc),
        grid=(t // tm,),
        in_specs=[pl.BlockSpec((tm, d), lambda i: (i, 0)),
                  pl.BlockSpec((1, 3, d), lambda i: (i // per_b, 0, 0)),
                  _resident((1, d)),
                  _resident((d, IN_COLS))],
        out_specs=[pl.BlockSpec((tm, MIX_COLS), lambda i: (i, 0)),
                   pl.BlockSpec((tm, GATE_COLS), lambda i: (i, 0))],
        out_shape=[jax.ShapeDtypeStruct((t, MIX_COLS), F32),
                   jax.ShapeDtypeStruct((t, GATE_COLS), BF16)],
        compiler_params=_cparams(("parallel",)),
        name="mix_inproj",
    )(x, mod, g, w_in)


def _hgrn_kernel(q_ref, f_ref, i_ref, g_ref, lb_ref, ng_ref, o_ref,
                 state_ref, qs_ref, ks_ref, bs_ref, oi_ref, *, ts):
    ck, kd, vd = A_CHUNK, A_KDIM, A_VDIM
    nc = ts // ck
    mid = ck // 2 - 1

    @pl.when(pl.program_id(2) == 0)
    def _():
        state_ref[...] = jnp.zeros_like(state_ref)

    def iota(shape, axis):
        return lax.broadcasted_iota(jnp.int32, shape, axis)

    tri2 = (iota((ck, 2 * ck), 0) >= iota((ck, 2 * ck), 1) % ck).astype(BF16)
    row1 = iota((ck, 1), 0)
    ng = ng_ref[...]
    ng2 = jnp.concatenate([ng, ng], axis=1)
    rows_of = lambda c: slice(c * ck, (c + 1) * ck)

    qraw = q_ref[0]
    fr = f_ref[0]
    lb = lb_ref[...]
    qs = qraw * _sigmoid(qraw)
    e = jnp.exp(-jnp.abs(fr))
    big = 1.0 / (1.0 + e)
    small = e * big
    pos = fr >= 0.0
    f = lb + (1.0 - lb) * jnp.where(pos, big, small)
    k = (1.0 - lb) * jnp.where(pos, small, big)
    lf_hi, lf_lo = _split_bf16(jnp.log(jnp.maximum(f, TINY)))
    b = jnp.concatenate(
        [_dot(tri2, jnp.concatenate([lf_hi[rows_of(c)], lf_lo[rows_of(c)]], axis=0)) for c in range(nc)],
        axis=0)
    worst = None
    for c in range(nc):
        b_mid = b[c * ck + mid:c * ck + mid + 1]
        b_end = b[(c + 1) * ck - 1:(c + 1) * ck]
        w = jnp.minimum(b_mid, b_end - b_mid)
        worst = w if worst is None else jnp.minimum(worst, w)
    safe = jnp.min(worst) > -HGRN_SAFE_SPAN

    @pl.when(safe)
    def _():
        key_head0 = iota((ck, 2 * kd), 1) < kd
        val_head0 = iota((ck, 2 * vd), 1) < vd
        causal2 = iota((ck, 2 * ck), 0) >= iota((ck, 2 * ck), 1) % ck
        same_head_state = (iota((2 * vd, 2 * kd), 0) < vd) == (iota((2 * vd, 2 * kd), 1) < kd)
        r4 = iota((4 * vd, 2 * vd), 0) % (2 * vd)
        same_head_sum = ((r4 < vd) == (iota((4 * vd, 2 * vd), 1) < vd)).astype(BF16)

        def block_diag(x, head0):
            zero = jnp.zeros_like(x)
            return jnp.concatenate([jnp.where(head0, x, zero), jnp.where(head0, zero, x)], axis=0)

        qa, ka, q_dec, k_end, end_decay, vb = {}, {}, {}, {}, {}, {}
        for c in range(nc):
            bc = b[rows_of(c)]
            b_mid = bc[mid:mid + 1]
            b_end = bc[ck - 1:ck]
            qa_f = qs[rows_of(c)] * jnp.exp(bc - b_mid)
            ka_f = k[rows_of(c)] * jnp.exp(b_mid - bc)
            qa[c] = qa_f.astype(BF16)
            ka[c] = block_diag(ka_f.astype(BF16), key_head0)
            q_dec[c] = (qa_f * jnp.exp(b_mid)).astype(BF16)
            k_end[c] = (ka_f * jnp.exp(b_end - b_mid)).astype(BF16)
            end_decay[c] = jnp.exp(b_end)
            vb[c] = i_ref[0, rows_of(c), :].astype(BF16)
        scores = {c: jnp.where(causal2, _dot_nt(qa[c], ka[c]), 0.0).astype(BF16) for c in range(nc)}
        o_intra = {c: _dot(scores[c], block_diag(vb[c], val_head0)) for c in range(nc)}
        update = {c: lax.dot_general(vb[c], k_end[c], (((0,), (0,)), ((), ())), preferred_element_type=F32)
                  for c in range(nc)}
        state_before = {}
        st = state_ref[...]
        for c in range(nc):
            state_before[c] = st.astype(BF16)
            st = end_decay[c] * st + jnp.where(same_head_state, update[c], 0.0)
        state_ref[...] = st
        outs = {c: o_intra[c] + _dot_nt(q_dec[c], state_before[c]) for c in range(nc)}
        squares = {}
        for c in range(nc):
            hi, lo = _split_bf16(outs[c] * outs[c])
            squares[c] = _dot(jnp.concatenate([hi, lo], axis=1), same_head_sum)
        for c in range(nc):
            gate = g_ref[0, rows_of(c), :]
            y = outs[c] * lax.rsqrt(squares[c] * (1.0 / vd) + EPS) * ng2
            o_ref[0, rows_of(c), :] = (y * (gate * _sigmoid(gate))).astype(o_ref.dtype)

    @pl.when(jnp.logical_not(safe))
    def _():
        for j in range(2):
            ksl = slice(j * kd, (j + 1) * kd)
            vsl = slice(j * vd, (j + 1) * vd)
            qs_ref[j] = qs[:, ksl]
            ks_ref[j] = k[:, ksl]
            bs_ref[j] = b[:, ksl]

            def chunk(c, carry):
                r0 = pl.multiple_of(c * ck, ck)
                rows = pl.ds(r0, ck)
                bc = bs_ref[j, rows, :]
                qc = qs_ref[j, rows, :]
                kc = ks_ref[j, rows, :]
                v = i_ref[0, rows, vsl]
                vb = v.astype(BF16)

                def one_row(t, cr):
                    bt = bs_ref[j, pl.ds(r0 + t, 1), :]
                    qt = qs_ref[j, pl.ds(r0 + t, 1), :]
                    dec = jnp.exp(jnp.where(row1 <= t, bt - bc, NEG_BIG))
                    sc = jnp.sum(dec * qt * kc, axis=-1, keepdims=True)
                    oi_ref[pl.ds(t, 1), :] = jnp.sum(sc * v, axis=0, keepdims=True)
                    return cr

                lax.fori_loop(0, ck, one_row, 0)
                b_end = bc[ck - 1:ck]
                st = state_ref[vsl, ksl]
                o = oi_ref[...] + _dot_nt((qc * jnp.exp(bc)).astype(BF16), st.astype(BF16))
                k_end = (kc * jnp.exp(b_end - bc)).astype(BF16)
                state_ref[vsl, ksl] = jnp.exp(b_end) * st + lax.dot_general(
                    vb, k_end, (((0,), (0,)), ((), ())), preferred_element_type=F32)
                gate = g_ref[0, rows, vsl]
                o_ref[0, rows, vsl] = (_rms(o, ng) * (gate * _sigmoid(gate))).astype(o_ref.dtype)
                return carry

            lax.fori_loop(0, nc, chunk, 0)


def _hgrn(mix3, lb, ng, ts=1024):
    bsz, seq, _ = mix3.shape
    pairs = A_HEADS // 2
    kw, vw = 2 * A_KDIM, 2 * A_VDIM
    return pl.pallas_call(
        functools.partial(_hgrn_kernel, ts=ts),
        grid=(bsz, pairs, seq // ts),
        in_specs=[pl.BlockSpec((1, ts, kw), lambda b, p, s: (b, s, OFF_AQ * LANES // kw + p)),
                  pl.BlockSpec((1, ts, kw), lambda b, p, s: (b, s, OFF_AF * LANES // kw + p)),
                  pl.BlockSpec((1, ts, vw), lambda b, p, s: (b, s, OFF_AI + p)),
                  pl.BlockSpec((1, ts, vw), lambda b, p, s: (b, s, OFF_AG + p)),
                  pl.BlockSpec((1, kw), lambda b, p, s: (0, p)),
                  pl.BlockSpec((1, A_VDIM), lambda b, p, s: (0, 0))],
        out_specs=pl.BlockSpec((1, ts, vw), lambda b, p, s: (b, s, p)),
        out_shape=jax.ShapeDtypeStruct((bsz, seq, A_V), BF16),
        scratch_shapes=[pltpu.VMEM((2 * A_VDIM, 2 * A_KDIM), F32),
                        pltpu.VMEM((2, ts, A_KDIM), F32),
                        pltpu.VMEM((2, ts, A_KDIM), F32),
                        pltpu.VMEM((2, ts, A_KDIM), F32),
                        pltpu.VMEM((A_CHUNK, A_VDIM), F32)],
        compiler_params=_cparams(("parallel", "parallel", "arbitrary")),
        name="hgrn2",
    )(mix3, mix3, mix3, mix3, lb, ng)


def _sb_kernel(q_ref, k_ref, v_ref, o_ref, run_ref, acc_ref):
    blk, d = B_BLOCK, B_HDIM
    seq = q_ref.shape[1]
    neg_scale = -(d ** -0.5)
    row = lax.broadcasted_iota(jnp.int32, (blk, 2 * blk), 0)
    col = lax.broadcasted_iota(jnp.int32, (blk, 2 * blk), 1)
    key_in_block = jnp.where(col >= blk, col - blk, col)
    strictly_before = key_in_block < row
    r2 = lax.broadcasted_iota(jnp.int32, (2 * blk, 2 * blk), 0)
    c2 = lax.broadcasted_iota(jnp.int32, (2 * blk, 2 * blk), 1)
    j2 = jnp.where(r2 >= blk, r2 - blk, r2)
    later_and_ones = jnp.logical_or(j2 > c2, c2 >= blk).astype(BF16)
    head0_lanes = lax.broadcasted_iota(jnp.int32, (blk, 2 * d), 1) < d

    def load_neg_q(q0):
        return (q_ref[0, pl.ds(q0, blk), :] * neg_scale).astype(BF16)

    def block_diag(x):
        zero = jnp.zeros_like(x)
        return jnp.concatenate([jnp.where(head0_lanes, x, zero), jnp.where(head0_lanes, zero, x)], axis=0)

    def logs(neg_z):
        log_1m = jnp.minimum(neg_z, 0.0) - jnp.log(1.0 + jnp.exp(-jnp.abs(neg_z)))
        return log_1m, log_1m - neg_z

    def later_sums(log_1m):
        hi, lo = _split_bf16(log_1m)
        per_head = [_dot(jnp.concatenate([hi[:, h * blk:(h + 1) * blk], lo[:, h * blk:(h + 1) * blk]], axis=1),
                         later_and_ones) for h in range(2)]
        within = jnp.concatenate([per_head[0][:, :blk], per_head[1][:, :blk]], axis=1)
        total = jnp.concatenate([per_head[0][:, blk:], per_head[1][:, blk:]], axis=1)
        return within, total

    def qgroup(qg, carry0):
        qbs = [qg * SB_QBLOCKS_PER_ITER + i for i in range(SB_QBLOCKS_PER_ITER)]
        q0s = [pl.multiple_of(qb * blk, blk) for qb in qbs]
        units = [(i, step) for i in range(SB_QBLOCKS_PER_ITER) for step in range(SB_FIXED_BLOCKS)]
        keeps, vals, neg_z = {}, {}, {}
        for i, step in units:
            kb = qbs[i] - step
            k0 = pl.multiple_of(jnp.maximum(kb, 0) * blk, blk)
            keeps[i, step] = strictly_before if step == 0 else kb >= 0
            vals[i, step] = block_diag(v_ref[0, pl.ds(k0, blk), :].astype(BF16))
            neg_z[i, step] = _dot_nt(load_neg_q(q0s[i]), block_diag(k_ref[0, pl.ds(k0, blk), :].astype(BF16)))
        log_bs, sums = {}, {}
        for u in units:
            log_1m, log_bs[u] = logs(neg_z[u])
            sums[u] = jnp.where(keeps[u], log_1m, 0.0)
        for u in units:
            sums[u] = later_sums(sums[u])
        weights = {}
        tops = {}
        for i in range(SB_QBLOCKS_PER_ITER):
            run = jnp.zeros((blk, 2 * blk), F32)
            for step in range(SB_FIXED_BLOCKS):
                u = (i, step)
                within, total = sums[u]
                weights[u] = jnp.where(keeps[u], jnp.exp(log_bs[u] + within + run), 0.0).astype(BF16)
                run = run + total
            run_ref[i] = run
            tops[i] = jnp.max(run)
        for i in range(SB_QBLOCKS_PER_ITER):
            acc = _dot(weights[i, 0], vals[i, 0])
            for step in range(1, SB_FIXED_BLOCKS):
                acc = acc + _dot(weights[i, step], vals[i, step])
            acc_ref[i] = acc

        def cond(st):
            kb, top = st
            return jnp.logical_and(kb >= 0, top > SB_SKIP_LOG)

        for i in range(SB_QBLOCKS_PER_ITER):
            def body(st, i=i):
                kb, _ = st
                k0 = pl.multiple_of(kb * blk, blk)
                log_1m, log_b = logs(_dot_nt(load_neg_q(q0s[i]),
                                             block_diag(k_ref[0, pl.ds(k0, blk), :].astype(BF16))))
                within, total = later_sums(log_1m)
                run = run_ref[i]
                a = jnp.exp(log_b + within + run)
                acc_ref[i] = acc_ref[i] + _dot(a.astype(BF16), block_diag(v_ref[0, pl.ds(k0, blk), :].astype(BF16)))
                run = run + total
                run_ref[i] = run
                return kb - 1, jnp.max(run)

            lax.while_loop(cond, body, (qbs[i] - SB_FIXED_BLOCKS, tops[i]))
            o_ref[0, pl.ds(q0s[i], blk), :] = acc_ref[i].astype(o_ref.dtype)
        return carry0

    lax.fori_loop(0, seq // (blk * SB_QBLOCKS_PER_ITER), qgroup, 0)


def _stick_breaking(mix3):
    bsz, seq, _ = mix3.shape
    pairs = B_HEADS // 2
    w = 2 * B_HDIM
    return pl.pallas_call(
        _sb_kernel,
        grid=(bsz, pairs),
        in_specs=[pl.BlockSpec((1, seq, w), lambda b, p: (b, 0, OFF_BQ + p)),
                  pl.BlockSpec((1, seq, w), lambda b, p: (b, 0, OFF_BK + p)),
                  pl.BlockSpec((1, seq, w), lambda b, p: (b, 0, OFF_BV + p))],
        out_specs=pl.BlockSpec((1, seq, w), lambda b, p: (b, 0, p)),
        out_shape=jax.ShapeDtypeStruct((bsz, seq, B_W), BF16),
        scratch_shapes=[pltpu.VMEM((SB_QBLOCKS_PER_ITER, B_BLOCK, 2 * B_BLOCK), F32),
                        pltpu.VMEM((SB_QBLOCKS_PER_ITER, B_BLOCK, 2 * B_HDIM), F32)],
        compiler_params=_cparams(("parallel", "parallel")),
        name="stick_breaking",
    )(mix3, mix3, mix3)


def _dil_kernel(*refs, tile, slopes):
    blk, d = C_BLOCK, C_HDIM
    n_groups = len(C_GROUPS)
    o_ref, os_ref, ls_ref = refs[5 * n_groups:]
    pair = pl.program_id(1)
    first_tile = pl.program_id(2) == 0
    scale = d ** -0.5
    qi = lax.broadcasted_iota(jnp.int32, (blk, 4 * blk), 0)
    lane4 = lax.broadcasted_iota(jnp.int32, (blk, 4 * blk), 1)
    kj = lane4 % (2 * blk)
    head0_keys = lane4 < 2 * blk
    delta = qi + blk - kj
    head0_lanes = lax.broadcasted_iota(jnp.int32, (2 * blk, 2 * d), 1) < d
    head0_out = lax.broadcasted_iota(jnp.int32, (blk, 2 * d), 1) < d
    ones0 = head0_lanes.astype(BF16)
    ones1 = jnp.logical_not(head0_lanes).astype(BF16)

    def block_diag_keys(k2):
        zero = jnp.zeros_like(k2)
        return jnp.concatenate([jnp.where(head0_lanes, k2, zero), jnp.where(head0_lanes, zero, k2)], axis=0)

    def block_diag_values(v2):
        zero = jnp.zeros_like(v2)
        return jnp.concatenate([jnp.concatenate([jnp.where(head0_lanes, v2, zero), ones0], axis=1),
                                jnp.concatenate([jnp.where(head0_lanes, zero, v2), ones1], axis=1)], axis=0)

    for g, (window, dil) in enumerate(C_GROUPS):
        q_ref, k_ref, kp_ref, v_ref, vp_ref = refs[5 * g:5 * g + 5]
        window_ok = jnp.logical_and(delta >= 0, delta <= window // dil)
        window_ok_first = jnp.logical_and(window_ok, jnp.logical_or(kj >= blk, jnp.logical_not(first_tile)))
        slope0 = jnp.where(pair == 0, jnp.float32(slopes[g][0]), jnp.float32(slopes[g][2]))
        slope1 = jnp.where(pair == 0, jnp.float32(slopes[g][1]), jnp.float32(slopes[g][3]))
        alibi = jnp.where(head0_keys, slope0, slope1) * (delta * dil).astype(F32)

        def strided(ref, rho, block):
            start = rho + block * blk * dil
            return ref[0, pl.ds(start, blk, stride=dil), :] if dil > 1 else ref[0, pl.ds(start, blk), :]

        blocks = [(rho, i) for rho in range(dil) for i in range(tile // (dil * blk))]
        for at in range(0, len(blocks), DIL_BATCH):
            batch = blocks[at:at + DIL_BATCH]
            scores, values = {}, {}
            for rho, i in batch:
                k2 = jnp.concatenate([strided(kp_ref, rho, 0) if i == 0 else strided(k_ref, rho, i - 1),
                                      strided(k_ref, rho, i)], axis=0).astype(BF16)
                v2 = jnp.concatenate([strided(vp_ref, rho, 0) if i == 0 else strided(v_ref, rho, i - 1),
                                      strided(v_ref, rho, i)], axis=0).astype(BF16)
                scores[rho, i] = _dot_nt((strided(q_ref, rho, i) * scale).astype(BF16), block_diag_keys(k2))
                values[rho, i] = block_diag_values(v2)
            probs, tops = {}, {}
            for rho, i in batch:
                valid = window_ok_first if i == 0 else window_ok
                s = jnp.where(valid, scores[rho, i] - alibi, NEG_BIG)
                mx0 = jnp.max(s[:, :2 * blk], axis=-1, keepdims=True)
                mx1 = jnp.max(s[:, 2 * blk:], axis=-1, keepdims=True)
                probs[rho, i] = jnp.exp(s - jnp.where(head0_keys, mx0, mx1)).astype(BF16)
                tops[rho, i] = jnp.where(head0_out, mx0, mx1)
            for rho, i in batch:
                r = _dot(probs[rho, i], values[rho, i])
                den = r[:, 2 * d:]
                start = rho + i * blk * dil
                rows = pl.ds(start, blk, stride=dil) if dil > 1 else pl.ds(start, blk)
                os_ref[g, rows, :] = r[:, :2 * d] / den
                ls_ref[g, rows, :] = tops[rho, i] + jnp.log(den)
    lses = [ls_ref[g] for g in range(n_groups)]
    top = functools.reduce(jnp.maximum, lses)
    wts = [jnp.exp(l - top) for l in lses]
    num = functools.reduce(lambda a, b: a + b, [wts[g] * os_ref[g] for g in range(n_groups)])
    o_ref[0] = (num / functools.reduce(lambda a, b: a + b, wts)).astype(o_ref.dtype)


def _dilated(mix3):
    bsz, seq, _ = mix3.shape
    pairs = C_HEADS_PER_GROUP // 2
    w = 2 * C_HDIM
    tile = C_BLOCK * max(dil for _, dil in C_GROUPS)
    all_slopes = _alibi_slopes(C_HEADS)
    slopes = tuple(tuple(all_slopes[g * C_HEADS_PER_GROUP:(g + 1) * C_HEADS_PER_GROUP])
                   for g in range(len(C_GROUPS)))
    in_specs = []
    for g, (_, dil) in enumerate(C_GROUPS):
        back = C_BLOCK * dil
        per = tile // back

        def cur(off, g=g):
            return pl.BlockSpec((1, tile, w), lambda b, p, n: (b, n, off + pairs * g + p))

        def prev(off, g=g, back=back, per=per):
            return pl.BlockSpec((1, back, w), lambda b, p, n: (b, jnp.maximum(n * per - 1, 0), off + pairs * g + p))

        in_specs += [cur(OFF_CQ), cur(OFF_CK), prev(OFF_CK), cur(OFF_CV), prev(OFF_CV)]
    return pl.pallas_call(
        functools.partial(_dil_kernel, tile=tile, slopes=slopes),
        grid=(bsz, pairs, seq // tile),
        in_specs=in_specs,
        out_specs=pl.BlockSpec((1, tile, w), lambda b, p, n: (b, n, p)),
        out_shape=jax.ShapeDtypeStruct((bsz, seq, C_OUT), BF16),
        scratch_shapes=[pltpu.VMEM((len(C_GROUPS), tile, w), F32),
                        pltpu.VMEM((len(C_GROUPS), tile, w), F32)],
        compiler_params=_cparams(("parallel", "parallel", "arbitrary")),
        name="dilated",
    )(*([mix3] * (5 * len(C_GROUPS))))


def _mixout_kernel(x_ref, mod_ref, g_ref, ya_ref, yb_ref, yc_ref, gate_ref, wa_ref, wb_ref, wc_ref, wo_ref,
                   out_ref):
    d = D_MODEL
    merged = gate_ref[:, 0:d].astype(F32) * _dot(ya_ref[...], wa_ref[...])
    merged = merged + gate_ref[:, d:2 * d].astype(F32) * _dot(yb_ref[...], wb_ref[...])
    merged = merged + gate_ref[:, 2 * d:3 * d].astype(F32) * _dot(yc_ref[...], wc_ref[...])
    y = _dot(merged.astype(BF16), wo_ref[...])
    out_ref[...] = x_ref[...] + mod_ref[0][2:3] * _rms(y, g_ref[...])


def _mixout(x, mod, g, ya, yb, yc, gates, wa, wb, wc, wo, seq, tm=512):
    t, d = x.shape
    per_b = seq // tm

    def tok(wd):
        return pl.BlockSpec((tm, wd), lambda i: (i, 0))

    return pl.pallas_call(
        _mixout_kernel,
        grid=(t // tm,),
        in_specs=[tok(d), pl.BlockSpec((1, 3, d), lambda i: (i // per_b, 0, 0)), _resident((1, d)),
                  tok(A_V), tok(B_W), tok(C_OUT), tok(GATE_COLS),
                  _resident(wa.shape), _resident(wb.shape), _resident(wc.shape), _resident(wo.shape)],
        out_specs=tok(d),
        out_shape=jax.ShapeDtypeStruct((t, d), F32),
        compiler_params=_cparams(("parallel",)),
        name="mix_out",
    )(x, mod, g, ya, yb, yc, gates, wa, wb, wc, wo)


def kernel(x, c, w_ada, b_ada, norm_g, ffn1_w_in, ffn1_w_out, w_in, hgrn_lb_logits, hgrn_norm_g,
           w_branch_a, w_branch_b, w_branch_c, w_out, ffn2_w_in, ffn2_w_out):
    bsz, seq, d = x.shape
    depth = w_ada.shape[0]
    lb_all = _lower_bounds(hgrn_lb_logits.astype(F32))
    mod = _ada(c, w_ada, b_ada).reshape(depth, bsz, 3, 3, d)
    xt = x.reshape(bsz * seq, d)
    for l in range(depth):
        xt = _ffn_sublayer(xt, mod[l, :, 0], norm_g[l, 0:2], ffn1_w_in[l].astype(BF16),
                           ffn1_w_out[l].astype(BF16), 0.5, seq)
        mix, gates = _inproj(xt, mod[l, :, 1], norm_g[l, 2:3], w_in[l].astype(BF16), seq)
        mix3 = mix.reshape(bsz, seq, MIX_COLS)
        ya = _hgrn(mix3, lb_all[l:l + 1], hgrn_norm_g[l:l + 1]).reshape(bsz * seq, A_V)
        yb = _stick_breaking(mix3).reshape(bsz * seq, B_W)
        yc = _dilated(mix3).reshape(bsz * seq, C_OUT)
        xt = _mixout(xt, mod[l, :, 1], norm_g[l, 3:4], ya, yb, yc, gates,
                     w_branch_a[l].astype(BF16), w_branch_b[l].astype(BF16), w_branch_c[l].astype(BF16),
                     w_out[l].astype(BF16), seq)
        xt = _ffn_sublayer(xt, mod[l, :, 2], norm_g[l, 4:6], ffn2_w_in[l].astype(BF16),
                           ffn2_w_out[l].astype(BF16), 0.5, seq)
    return xt.reshape(bsz, seq, d)
```

```python
import functools
import math

import jax
import jax.numpy as jnp
from jax import lax
from jax.experimental import pallas as pl
from jax.experimental.pallas import tpu as pltpu

D_MODEL = 1024
DEPTH = 4
A_HEADS = 6
A_KDIM = 128
A_VDIM = 64
A_CHUNK = 64
B_HEADS = 6
B_HDIM = 64
B_BLOCK = 128
C_GROUPS = ((128, 1), (512, 4), (2048, 16))
C_HEADS_PER_GROUP = 4
C_HEADS = C_HEADS_PER_GROUP * len(C_GROUPS)
C_HDIM = 64
C_BLOCK = 128
D_FF = 2816
N_BRANCH = 3
EPS = 1e-6
NEG_BIG = -1e30
TINY = 1e-30

A_QK = A_HEADS * A_KDIM
A_V = A_HEADS * A_VDIM
B_W = B_HEADS * B_HDIM
C_W = C_HEADS * C_HDIM
C_OUT = C_HEADS_PER_GROUP * C_HDIM
MIX_COLS = 2 * A_QK + 2 * A_V + 3 * B_W + 3 * C_W
GATE_COLS = N_BRANCH * D_MODEL
IN_COLS = MIX_COLS + GATE_COLS

LANES = 128
MXU_WIDTH = 256
OFF_AQ = 0
OFF_AF = A_QK // LANES
OFF_AI = 2 * A_QK // LANES
OFF_AG = (2 * A_QK + A_V) // LANES
OFF_BQ = (2 * A_QK + 2 * A_V) // LANES
OFF_BK = OFF_BQ + B_W // LANES
OFF_BV = OFF_BK + B_W // LANES
OFF_CQ = OFF_BV + B_W // LANES
OFF_CK = OFF_CQ + C_W // LANES
OFF_CV = OFF_CK + C_W // LANES
MIX_BLOCKS = MIX_COLS // LANES

VMEM_LIMIT = 56 * 1024 * 1024
SB_SKIP_LOG = -105.0
LOG2_E = 1.4426950408889634
SB_FIXED_BLOCKS = 3
SB_QBLOCKS_PER_ITER = 2
DIL_BATCH = 4
HGRN_SAFE_SPAN = 80.0

F32 = jnp.float32
BF16 = jnp.bfloat16


def _alibi_slopes(n):
    def pow2_slopes(m):
        start = 2.0 ** (-8.0 / m)
        return [start ** (i + 1) for i in range(m)]
    if math.log2(n).is_integer():
        s = pow2_slopes(n)
    else:
        c = 2 ** int(math.floor(math.log2(n)))
        s = pow2_slopes(c) + pow2_slopes(2 * c)[0::2][: n - c]
    return sorted(s, reverse=True)


def _rms(x, g):
    return x * lax.rsqrt(jnp.mean(x * x, axis=-1, keepdims=True) + EPS) * g


def _sigmoid(x):
    return 1.0 / (1.0 + jnp.exp(-x))


def _dot(a, b):
    return jnp.dot(a, b, preferred_element_type=F32)


def _dot_nt(a, b):
    return lax.dot_general(a, b, (((1,), (1,)), ((), ())), preferred_element_type=F32)


def _split_bf16(x):
    hi = x.astype(BF16)
    lo = (x - hi.astype(F32)).astype(BF16)
    return hi, lo


def _cparams(sem):
    return pltpu.CompilerParams(dimension_semantics=sem, vmem_limit_bytes=VMEM_LIMIT)


def _resident(shape):
    nd = len(shape)
    return pl.BlockSpec(shape, lambda *_: (0,) * nd, pipeline_mode=pl.Buffered(1))


def _ada_kernel(c_ref, w_ref, b_ref, o_ref):
    c = c_ref[...]
    ca = c * _sigmoid(c)
    o_ref[0] = jnp.dot(ca, w_ref[0], precision=lax.Precision.HIGHEST,
                       preferred_element_type=F32) + b_ref[0]


def _ada(c, w_ada, b_ada):
    depth, d, n = w_ada.shape
    bsz = c.shape[0]
    tn = 1152
    return pl.pallas_call(
        _ada_kernel,
        grid=(depth, n // tn),
        in_specs=[pl.BlockSpec((bsz, d), lambda l, j: (0, 0)),
                  pl.BlockSpec((1, d, tn), lambda l, j: (l, 0, j)),
                  pl.BlockSpec((1, 1, tn), lambda l, j: (l, 0, j))],
        out_specs=pl.BlockSpec((1, bsz, tn), lambda l, j: (l, 0, j)),
        out_shape=jax.ShapeDtypeStruct((depth, bsz, n), F32),
        compiler_params=_cparams(("parallel", "parallel")),
        name="ada_mod",
    )(c, w_ada, b_ada.reshape(depth, 1, n))


def _lb_kernel(l_ref, o_ref):
    x = l_ref[...]
    e = jnp.exp(x - jnp.max(x, axis=0, keepdims=True))
    p = e / jnp.sum(e, axis=0, keepdims=True)
    depth = x.shape[0]
    run = p[0:1]
    o_ref[0:1, :] = run - p[0:1]
    for l in range(1, depth):
        run = run + p[l:l + 1]
        o_ref[l:l + 1, :] = run - p[0:1]


def _lower_bounds(logits):
    return pl.pallas_call(_lb_kernel, out_shape=jax.ShapeDtypeStruct(logits.shape, F32),
                          name="hgrn_lower_bounds")(logits)


def _ffn_kernel(x_ref, mod_ref, g_ref, win_ref, wout_ref, o_ref, act_ref, *, res_w, ffc, sub):
    m = mod_ref[0]
    d_ff = wout_ref.shape[0]
    subs = [slice(s * sub, (s + 1) * sub) for s in range(x_ref.shape[0] // sub)]
    hb = [(_rms(x_ref[rs, :], g_ref[0:1]) * (1.0 + m[1:2]) + m[0:1]).astype(BF16) for rs in subs]
    for s, rs in enumerate(subs):
        for c in range(d_ff // ffc):
            a = _dot(hb[s], win_ref[:, c * ffc:(c + 1) * ffc])
            b = _dot(hb[s], win_ref[:, d_ff + c * ffc:d_ff + (c + 1) * ffc])
            act_ref[rs, c * ffc:(c + 1) * ffc] = (a * _sigmoid(a) * b).astype(BF16)
    ys = [_dot(act_ref[rs, :], wout_ref[...]) for rs in subs]
    for s, rs in enumerate(subs):
        o_ref[rs, :] = x_ref[rs, :] + res_w * m[2:3] * _rms(ys[s], g_ref[1:2])


def _ffn_sublayer(x, mod, g2, w_in, w_out, res_w, seq, tm=1024, ffc=MXU_WIDTH, sub=512):
    t, d = x.shape
    d_ff = w_out.shape[0]
    per_b = seq // tm
    return pl.pallas_call(
        functools.partial(_ffn_kernel, res_w=res_w, ffc=ffc, sub=sub),
        grid=(t // tm,),
        in_specs=[pl.BlockSpec((tm, d), lambda i: (i, 0)),
                  pl.BlockSpec((1, 3, d), lambda i: (i // per_b, 0, 0)),
                  _resident((2, d)),
                  _resident((d, 2 * d_ff)),
                  _resident((d_ff, d))],
        out_specs=pl.BlockSpec((tm, d), lambda i: (i, 0)),
        out_shape=jax.ShapeDtypeStruct((t, d), F32),
        scratch_shapes=[pltpu.VMEM((tm, d_ff), BF16)],
        compiler_params=_cparams(("parallel",)),
        name="ffn_sublayer",
    )(x, mod, g2, w_in, w_out)


def _inproj_kernel(x_ref, mod_ref, g_ref, w_ref, mix_ref, gate_ref, *, cc):
    x = x_ref[...]
    m = mod_ref[0]
    h = _rms(x, g_ref[...]) * (1.0 + m[1:2]) + m[0:1]
    hb = h.astype(BF16)
    for lo in range(0, IN_COLS, cc):
        hi = min(lo + cc, IN_COLS)
        z = _dot(hb, w_ref[:, lo:hi])
        mix_hi = min(hi, MIX_COLS)
        if lo < mix_hi:
            mix_ref[:, lo:mix_hi] = z[:, :mix_hi - lo]
        if hi > MIX_COLS:
            g_lo = max(lo, MIX_COLS)
            gate_ref[:, g_lo - MIX_COLS:hi - MIX_COLS] = _sigmoid(z[:, g_lo - lo:]).astype(BF16)


def _inproj(x, mod, g, w_in, seq, tm=512, cc=MXU_WIDTH):
    t, d = x.shape
    per_b = seq // tm
    return pl.pallas_call(
        functools.partial(_inproj_kernel, cc=cc),
        grid=(t // tm,),
        in_specs=[pl.BlockSpec((tm, d), lambda i: (i, 0)),
                  pl.BlockSpec((1, 3, d), lambda i: (i // per_b, 0, 0)),
                  _resident((1, d)),
                  _resident((d, IN_COLS))],
        out_specs=[pl.BlockSpec((tm, MIX_COLS), lambda i: (i, 0)),
                   pl.BlockSpec((tm, GATE_COLS), lambda i: (i, 0))],
        out_shape=[jax.ShapeDtypeStruct((t, MIX_COLS), F32),
                   jax.ShapeDtypeStruct((t, GATE_COLS), BF16)],
        compiler_params=_cparams(("parallel",)),
        name="mix_inproj",
    )(x, mod, g, w_in)


def _hgrn_kernel(q_ref, f_ref, i_ref, g_ref, lb_ref, ng_ref, o_ref,
                 state_ref, qs_ref, ks_ref, bs_ref, oi_ref, *, ts):
    ck, kd, vd = A_CHUNK, A_KDIM, A_VDIM
    nc = ts // ck
    mid = ck // 2 - 1

    @pl.when(pl.program_id(2) == 0)
    def _():
        state_ref[...] = jnp.zeros_like(state_ref)

    def iota(shape, axis):
        return lax.broadcasted_iota(jnp.int32, shape, axis)

    tri2 = (iota((ck, 2 * ck), 0) >= iota((ck, 2 * ck), 1) % ck).astype(BF16)
    row1 = iota((ck, 1), 0)
    ng = ng_ref[...]
    ng2 = jnp.concatenate([ng, ng], axis=1)
    rows_of = lambda c: slice(c * ck, (c + 1) * ck)

    qraw = q_ref[0]
    fr = f_ref[0]
    lb = lb_ref[...]
    qs = qraw * _sigmoid(qraw)
    e = jnp.exp(-jnp.abs(fr))
    big = 1.0 / (1.0 + e)
    small = e * big
    pos = fr >= 0.0
    f = lb + (1.0 - lb) * jnp.where(pos, big, small)
    k = (1.0 - lb) * jnp.where(pos, small, big)
    lf_hi, lf_lo = _split_bf16(jnp.log(jnp.maximum(f, TINY)))
    b = jnp.concatenate(
        [_dot(tri2, jnp.concatenate([lf_hi[rows_of(c)], lf_lo[rows_of(c)]], axis=0)) for c in range(nc)],
        axis=0)
    worst = None
    for c in range(nc):
        b_mid = b[c * ck + mid:c * ck + mid + 1]
        b_end = b[(c + 1) * ck - 1:(c + 1) * ck]
        w = jnp.minimum(b_mid, b_end - b_mid)
        worst = w if worst is None else jnp.minimum(worst, w)
    safe = jnp.min(worst) > -HGRN_SAFE_SPAN

    @pl.when(safe)
    def _():
        key_head0 = iota((ck, 2 * kd), 1) < kd
        val_head0 = iota((ck, 2 * vd), 1) < vd
        causal2 = iota((ck, 2 * ck), 0) >= iota((ck, 2 * ck), 1) % ck
        same_head_state = (iota((2 * vd, 2 * kd), 0) < vd) == (iota((2 * vd, 2 * kd), 1) < kd)
        r4 = iota((4 * vd, 2 * vd), 0) % (2 * vd)
        same_head_sum = ((r4 < vd) == (iota((4 * vd, 2 * vd), 1) < vd)).astype(BF16)

        def block_diag(x, head0):
            zero = jnp.zeros_like(x)
            return jnp.concatenate([jnp.where(head0, x, zero), jnp.where(head0, zero, x)], axis=0)

        qa, ka, q_dec, k_end, end_decay, vb = {}, {}, {}, {}, {}, {}
        for c in range(nc):
            bc = b[rows_of(c)]
            b_mid = bc[mid:mid + 1]
            b_end = bc[ck - 1:ck]
            qa_f = qs[rows_of(c)] * jnp.exp(bc - b_mid)
            ka_f = k[rows_of(c)] * jnp.exp(b_mid - bc)
            qa[c] = qa_f.astype(BF16)
            ka[c] = block_diag(ka_f.astype(BF16), key_head0)
            q_dec[c] = (qa_f * jnp.exp(b_mid)).astype(BF16)
            k_end[c] = (ka_f * jnp.exp(b_end - b_mid)).astype(BF16)
            end_decay[c] = jnp.exp(b_end)
            vb[c] = i_ref[0, rows_of(c), :].astype(BF16)
        scores = {c: jnp.where(causal2, _dot_nt(qa[c], ka[c]), 0.0).astype(BF16) for c in range(nc)}
        o_intra = {c: _dot(scores[c], block_diag(vb[c], val_head0)) for c in range(nc)}
        update = {c: lax.dot_general(vb[c], k_end[c], (((0,), (0,)), ((), ())), preferred_element_type=F32)
                  for c in range(nc)}
        state_before = {}
        st = state_ref[...]
        for c in range(nc):
            state_before[c] = st.astype(BF16)
            st = end_decay[c] * st + jnp.where(same_head_state, update[c], 0.0)
        state_ref[...] = st
        outs = {c: o_intra[c] + _dot_nt(q_dec[c], state_before[c]) for c in range(nc)}
        squares = {}
        for c in range(nc):
            hi, lo = _split_bf16(outs[c] * outs[c])
            squares[c] = _dot(jnp.concatenate([hi, lo], axis=1), same_head_sum)
        for c in range(nc):
            gate = g_ref[0, rows_of(c), :]
            y = outs[c] * lax.rsqrt(squares[c] * (1.0 / vd) + EPS) * ng2
            o_ref[0, rows_of(c), :] = (y * (gate * _sigmoid(gate))).astype(o_ref.dtype)

    @pl.when(jnp.logical_not(safe))
    def _():
        for j in range(2):
            ksl = slice(j * kd, (j + 1) * kd)
            vsl = slice(j * vd, (j + 1) * vd)
            qs_ref[j] = qs[:, ksl]
            ks_ref[j] = k[:, ksl]
            bs_ref[j] = b[:, ksl]

            def chunk(c, carry):
                r0 = pl.multiple_of(c * ck, ck)
                rows = pl.ds(r0, ck)
                bc = bs_ref[j, rows, :]
                qc = qs_ref[j, rows, :]
                kc = ks_ref[j, rows, :]
                v = i_ref[0, rows, vsl]
                vb = v.astype(BF16)

                def one_row(t, cr):
                    bt = bs_ref[j, pl.ds(r0 + t, 1), :]
                    qt = qs_ref[j, pl.ds(r0 + t, 1), :]
                    dec = jnp.exp(jnp.where(row1 <= t, bt - bc, NEG_BIG))
                    sc = jnp.sum(dec * qt * kc, axis=-1, keepdims=True)
                    oi_ref[pl.ds(t, 1), :] = jnp.sum(sc * v, axis=0, keepdims=True)
                    return cr

                lax.fori_loop(0, ck, one_row, 0)
                b_end = bc[ck - 1:ck]
                st = state_ref[vsl, ksl]
                o = oi_ref[...] + _dot_nt((qc * jnp.exp(bc)).astype(BF16), st.astype(BF16))
                k_end = (kc * jnp.exp(b_end - bc)).astype(BF16)
                state_ref[vsl, ksl] = jnp.exp(b_end) * st + lax.dot_general(
                    vb, k_end, (((0,), (0,)), ((), ())), preferred_element_type=F32)
                gate = g_ref[0, rows, vsl]
                o_ref[0, rows, vsl] = (_rms(o, ng) * (gate * _sigmoid(gate))).astype(o_ref.dtype)
                return carry

            lax.fori_loop(0, nc, chunk, 0)


def _hgrn(mix3, lb, ng, ts=1024):
    bsz, seq, _ = mix3.shape
    pairs = A_HEADS // 2
    kw, vw = 2 * A_KDIM, 2 * A_VDIM
    return pl.pallas_call(
        functools.partial(_hgrn_kernel, ts=ts),
        grid=(bsz, pairs, seq // ts),
        in_specs=[pl.BlockSpec((1, ts, kw), lambda b, p, s: (b, s, OFF_AQ * LANES // kw + p)),
                  pl.BlockSpec((1, ts, kw), lambda b, p, s: (b, s, OFF_AF * LANES // kw + p)),
                  pl.BlockSpec((1, ts, vw), lambda b, p, s: (b, s, OFF_AI + p)),
                  pl.BlockSpec((1, ts, vw), lambda b, p, s: (b, s, OFF_AG + p)),
                  pl.BlockSpec((1, kw), lambda b, p, s: (0, p)),
                  pl.BlockSpec((1, A_VDIM), lambda b, p, s: (0, 0))],
        out_specs=pl.BlockSpec((1, ts, vw), lambda b, p, s: (b, s, p)),
        out_shape=jax.ShapeDtypeStruct((bsz, seq, A_V), BF16),
        scratch_shapes=[pltpu.VMEM((2 * A_VDIM, 2 * A_KDIM), F32),
                        pltpu.VMEM((2, ts, A_KDIM), F32),
                        pltpu.VMEM((2, ts, A_KDIM), F32),
                        pltpu.VMEM((2, ts, A_KDIM), F32),
                        pltpu.VMEM((A_CHUNK, A_VDIM), F32)],
        compiler_params=_cparams(("parallel", "parallel", "arbitrary")),
        name="hgrn2",
    )(mix3, mix3, mix3, mix3, lb, ng)


def _sb_kernel(q_ref, k_ref, v_ref, o_ref, run_ref, acc_ref):
    blk, d = B_BLOCK, B_HDIM
    seq = q_ref.shape[1]
    neg_scale = -(d ** -0.5) * LOG2_E
    row = lax.broadcasted_iota(jnp.int32, (blk, 2 * blk), 0)
    col = lax.broadcasted_iota(jnp.int32, (blk, 2 * blk), 1)
    key_in_block = jnp.where(col >= blk, col - blk, col)
    strictly_before = key_in_block < row
    r2 = lax.broadcasted_iota(jnp.int32, (2 * blk, 2 * blk), 0)
    c2 = lax.broadcasted_iota(jnp.int32, (2 * blk, 2 * blk), 1)
    j2 = jnp.where(r2 >= blk, r2 - blk, r2)
    later_and_ones = jnp.logical_or(j2 > c2, c2 >= blk).astype(BF16)
    head0_lanes = lax.broadcasted_iota(jnp.int32, (blk, 2 * d), 1) < d

    def load_neg_q(q0):
        return (q_ref[0, pl.ds(q0, blk), :] * neg_scale).astype(BF16)

    def block_diag(x):
        zero = jnp.zeros_like(x)
        return jnp.concatenate([jnp.where(head0_lanes, x, zero), jnp.where(head0_lanes, zero, x)], axis=0)

    def logs(neg_z):
        log_1m = jnp.minimum(neg_z, 0.0) - jnp.log(1.0 + jnp.exp2(-jnp.abs(neg_z))) * LOG2_E
        return log_1m, log_1m - neg_z

    def later_sums(log_1m):
        hi, lo = _split_bf16(log_1m)
        per_head = [_dot(jnp.concatenate([hi[:, h * blk:(h + 1) * blk], lo[:, h * blk:(h + 1) * blk]], axis=1),
                         later_and_ones) for h in range(2)]
        within = jnp.concatenate([per_head[0][:, :blk], per_head[1][:, :blk]], axis=1)
        total = jnp.concatenate([per_head[0][:, blk:], per_head[1][:, blk:]], axis=1)
        return within, total

    def qgroup(qg, carry0, edge):
        qbs = [qg * SB_QBLOCKS_PER_ITER + i for i in range(SB_QBLOCKS_PER_ITER)]
        q0s = [pl.multiple_of(qb * blk, blk) for qb in qbs]
        units = [(i, step) for i in range(SB_QBLOCKS_PER_ITER) for step in range(SB_FIXED_BLOCKS)]
        keeps, vals, neg_z = {}, {}, {}
        for i, step in units:
            kb = qbs[i] - step
            k0 = pl.multiple_of(jnp.maximum(kb, 0) * blk, blk)
            keeps[i, step] = strictly_before if step == 0 else (kb >= 0 if edge else None)
            vals[i, step] = block_diag(v_ref[0, pl.ds(k0, blk), :].astype(BF16))
            neg_z[i, step] = _dot_nt(load_neg_q(q0s[i]), block_diag(k_ref[0, pl.ds(k0, blk), :].astype(BF16)))
        log_bs, sums = {}, {}
        for u in units:
            log_1m, log_bs[u] = logs(neg_z[u])
            sums[u] = log_1m if keeps[u] is None else jnp.where(keeps[u], log_1m, 0.0)
        for u in units:
            sums[u] = later_sums(sums[u])
        weights = {}
        tops = {}
        for i in range(SB_QBLOCKS_PER_ITER):
            run = jnp.zeros((blk, 2 * blk), F32)
            for step in range(SB_FIXED_BLOCKS):
                u = (i, step)
                within, total = sums[u]
                a = jnp.exp2(log_bs[u] + within + run)
                weights[u] = (a if keeps[u] is None else jnp.where(keeps[u], a, 0.0)).astype(BF16)
                run = run + total
            run_ref[i] = run
            tops[i] = jnp.max(run)
        for i in range(SB_QBLOCKS_PER_ITER):
            acc = _dot(weights[i, 0], vals[i, 0])
            for step in range(1, SB_FIXED_BLOCKS):
                acc = acc + _dot(weights[i, step], vals[i, step])
            acc_ref[i] = acc

        def cond(st):
            kb, top = st
            return jnp.logical_and(kb >= 0, top > SB_SKIP_LOG * LOG2_E)

        for i in range(SB_QBLOCKS_PER_ITER):
            def body(st, i=i):
                kb, _ = st
                k0 = pl.multiple_of(kb * blk, blk)
                log_1m, log_b = logs(_dot_nt(load_neg_q(q0s[i]),
                                             block_diag(k_ref[0, pl.ds(k0, blk), :].astype(BF16))))
                within, total = later_sums(log_1m)
                run = run_ref[i]
                a = jnp.exp2(log_b + within + run)
                acc_ref[i] = acc_ref[i] + _dot(a.astype(BF16), block_diag(v_ref[0, pl.ds(k0, blk), :].astype(BF16)))
                run = run + total
                run_ref[i] = run
                return kb - 1, jnp.max(run)

            lax.while_loop(cond, body, (qbs[i] - SB_FIXED_BLOCKS, tops[i]))
            o_ref[0, pl.ds(q0s[i], blk), :] = acc_ref[i].astype(o_ref.dtype)
        return carry0

    first_full = -(-(SB_FIXED_BLOCKS - 1) // SB_QBLOCKS_PER_ITER)
    for qg in range(first_full):
        qgroup(qg, 0, True)
    lax.fori_loop(first_full, seq // (blk * SB_QBLOCKS_PER_ITER), functools.partial(qgroup, edge=False), 0)


def _stick_breaking(mix3):
    bsz, seq, _ = mix3.shape
    pairs = B_HEADS // 2
    w = 2 * B_HDIM
    return pl.pallas_call(
        _sb_kernel,
        grid=(bsz, pairs),
        in_specs=[pl.BlockSpec((1, seq, w), lambda b, p: (b, 0, OFF_BQ + p)),
                  pl.BlockSpec((1, seq, w), lambda b, p: (b, 0, OFF_BK + p)),
                  pl.BlockSpec((1, seq, w), lambda b, p: (b, 0, OFF_BV + p))],
        out_specs=pl.BlockSpec((1, seq, w), lambda b, p: (b, 0, p)),
        out_shape=jax.ShapeDtypeStruct((bsz, seq, B_W), BF16),
        scratch_shapes=[pltpu.VMEM((SB_QBLOCKS_PER_ITER, B_BLOCK, 2 * B_BLOCK), F32),
                        pltpu.VMEM((SB_QBLOCKS_PER_ITER, B_BLOCK, 2 * B_HDIM), F32)],
        compiler_params=_cparams(("parallel", "parallel")),
        name="stick_breaking",
    )(mix3, mix3, mix3)


def _dil_kernel(*refs, tile, slopes):
    blk, d = C_BLOCK, C_HDIM
    n_groups = len(C_GROUPS)
    o_ref, os_ref, ls_ref = refs[5 * n_groups:]
    pair = pl.program_id(1)
    first_tile = pl.program_id(2) == 0
    scale = d ** -0.5
    qi = lax.broadcasted_iota(jnp.int32, (blk, 4 * blk), 0)
    lane4 = lax.broadcasted_iota(jnp.int32, (blk, 4 * blk), 1)
    kj = lane4 % (2 * blk)
    head0_keys = lane4 < 2 * blk
    delta = qi + blk - kj
    head0_lanes = lax.broadcasted_iota(jnp.int32, (2 * blk, 2 * d), 1) < d
    head0_out = lax.broadcasted_iota(jnp.int32, (blk, 2 * d), 1) < d
    ones0 = head0_lanes.astype(BF16)
    ones1 = jnp.logical_not(head0_lanes).astype(BF16)

    def block_diag_keys(k2):
        zero = jnp.zeros_like(k2)
        return jnp.concatenate([jnp.where(head0_lanes, k2, zero), jnp.where(head0_lanes, zero, k2)], axis=0)

    def block_diag_values(v2):
        zero = jnp.zeros_like(v2)
        return jnp.concatenate([jnp.concatenate([jnp.where(head0_lanes, v2, zero), ones0], axis=1),
                                jnp.concatenate([jnp.where(head0_lanes, zero, v2), ones1], axis=1)], axis=0)

    for g, (window, dil) in enumerate(C_GROUPS):
        q_ref, k_ref, kp_ref, v_ref, vp_ref = refs[5 * g:5 * g + 5]
        window_ok = jnp.logical_and(delta >= 0, delta <= window // dil)
        window_ok_first = jnp.logical_and(window_ok, jnp.logical_or(kj >= blk, jnp.logical_not(first_tile)))
        slope0 = jnp.where(pair == 0, jnp.float32(slopes[g][0]), jnp.float32(slopes[g][2]))
        slope1 = jnp.where(pair == 0, jnp.float32(slopes[g][1]), jnp.float32(slopes[g][3]))
        alibi = jnp.where(head0_keys, slope0, slope1) * (delta * dil).astype(F32)

        def strided(ref, rho, block):
            start = rho + block * blk * dil
            return ref[0, pl.ds(start, blk, stride=dil), :] if dil > 1 else ref[0, pl.ds(start, blk), :]

        blocks = [(rho, i) for rho in range(dil) for i in range(tile // (dil * blk))]
        for at in range(0, len(blocks), DIL_BATCH):
            batch = blocks[at:at + DIL_BATCH]
            scores, values = {}, {}
            for rho, i in batch:
                k2 = jnp.concatenate([strided(kp_ref, rho, 0) if i == 0 else strided(k_ref, rho, i - 1),
                                      strided(k_ref, rho, i)], axis=0).astype(BF16)
                v2 = jnp.concatenate([strided(vp_ref, rho, 0) if i == 0 else strided(v_ref, rho, i - 1),
                                      strided(v_ref, rho, i)], axis=0).astype(BF16)
                scores[rho, i] = _dot_nt((strided(q_ref, rho, i) * scale).astype(BF16), block_diag_keys(k2))
                values[rho, i] = block_diag_values(v2)
            probs, tops = {}, {}
            for rho, i in batch:
                valid = window_ok_first if i == 0 else window_ok
                s = jnp.where(valid, scores[rho, i] - alibi, NEG_BIG)
                mx0 = jnp.max(s[:, :2 * blk], axis=-1, keepdims=True)
                mx1 = jnp.max(s[:, 2 * blk:], axis=-1, keepdims=True)
                probs[rho, i] = jnp.exp(s - jnp.where(head0_keys, mx0, mx1)).astype(BF16)
                tops[rho, i] = jnp.where(head0_out, mx0, mx1)
            for rho, i in batch:
                r = _dot(probs[rho, i], values[rho, i])
                den = r[:, 2 * d:]
                start = rho + i * blk * dil
                rows = pl.ds(start, blk, stride=dil) if dil > 1 else pl.ds(start, blk)
                os_ref[g, rows, :] = r[:, :2 * d] / den
                ls_ref[g, rows, :] = tops[rho, i] + jnp.log(den)
    lses = [ls_ref[g] for g in range(n_groups)]
    top = functools.reduce(jnp.maximum, lses)
    wts = [jnp.exp(l - top) for l in lses]
    num = functools.reduce(lambda a, b: a + b, [wts[g] * os_ref[g] for g in range(n_groups)])
    o_ref[0] = (num / functools.reduce(lambda a, b: a + b, wts)).astype(o_ref.dtype)


def _dilated(mix3):
    bsz, seq, _ = mix3.shape
    pairs = C_HEADS_PER_GROUP // 2
    w = 2 * C_HDIM
    tile = C_BLOCK * max(dil for _, dil in C_GROUPS)
    all_slopes = _alibi_slopes(C_HEADS)
    slopes = tuple(tuple(all_slopes[g * C_HEADS_PER_GROUP:(g + 1) * C_HEADS_PER_GROUP])
                   for g in range(len(C_GROUPS)))
    in_specs = []
    for g, (_, dil) in enumerate(C_GROUPS):
        back = C_BLOCK * dil
        per = tile // back

        def cur(off, g=g):
            return pl.BlockSpec((1, tile, w), lambda b, p, n: (b, n, off + pairs * g + p))

        def prev(off, g=g, back=back, per=per):
            return pl.BlockSpec((1, back, w), lambda b, p, n: (b, jnp.maximum(n * per - 1, 0), off + pairs * g + p))

        in_specs += [cur(OFF_CQ), cur(OFF_CK), prev(OFF_CK), cur(OFF_CV), prev(OFF_CV)]
    return pl.pallas_call(
        functools.partial(_dil_kernel, tile=tile, slopes=slopes),
        grid=(bsz, pairs, seq // tile),
        in_specs=in_specs,
        out_specs=pl.BlockSpec((1, tile, w), lambda b, p, n: (b, n, p)),
        out_shape=jax.ShapeDtypeStruct((bsz, seq, C_OUT), BF16),
        scratch_shapes=[pltpu.VMEM((len(C_GROUPS), tile, w), F32),
                        pltpu.VMEM((len(C_GROUPS), tile, w), F32)],
        compiler_params=_cparams(("parallel", "parallel", "arbitrary")),
        name="dilated",
    )(*([mix3] * (5 * len(C_GROUPS))))


def _mixout_kernel(x_ref, mod_ref, g_ref, ya_ref, yb_ref, yc_ref, gate_ref, wa_ref, wb_ref, wc_ref, wo_ref,
                   out_ref):
    d = D_MODEL
    merged = gate_ref[:, 0:d].astype(F32) * _dot(ya_ref[...], wa_ref[...])
    merged = merged + gate_ref[:, d:2 * d].astype(F32) * _dot(yb_ref[...], wb_ref[...])
    merged = merged + gate_ref[:, 2 * d:3 * d].astype(F32) * _dot(yc_ref[...], wc_ref[...])
    y = _dot(merged.astype(BF16), wo_ref[...])
    out_ref[...] = x_ref[...] + mod_ref[0][2:3] * _rms(y, g_ref[...])


def _mixout(x, mod, g, ya, yb, yc, gates, wa, wb, wc, wo, seq, tm=512):
    t, d = x.shape
    per_b = seq // tm

    def tok(wd):
        return pl.BlockSpec((tm, wd), lambda i: (i, 0))

    return pl.pallas_call(
        _mixout_kernel,
        grid=(t // tm,),
        in_specs=[tok(d), pl.BlockSpec((1, 3, d), lambda i: (i // per_b, 0, 0)), _resident((1, d)),
                  tok(A_V), tok(B_W), tok(C_OUT), tok(GATE_COLS),
                  _resident(wa.shape), _resident(wb.shape), _resident(wc.shape), _resident(wo.shape)],
        out_specs=tok(d),
        out_shape=jax.ShapeDtypeStruct((t, d), F32),
        compiler_params=_cparams(("parallel",)),
        name="mix_out",
    )(x, mod, g, ya, yb, yc, gates, wa, wb, wc, wo)


def kernel(x, c, w_ada, b_ada, norm_g, ffn1_w_in, ffn1_w_out, w_in, hgrn_lb_logits, hgrn_norm_g,
           w_branch_a, w_branch_b, w_branch_c, w_out, ffn2_w_in, ffn2_w_out):
    bsz, seq, d = x.shape
    depth = w_ada.shape[0]
    lb_all = _lower_bounds(hgrn_lb_logits.astype(F32))
    mod = _ada(c, w_ada, b_ada).reshape(depth, bsz, 3, 3, d)
    xt = x.reshape(bsz * seq, d)
    for l in range(depth):
        xt = _ffn_sublayer(xt, mod[l, :, 0], norm_g[l, 0:2], ffn1_w_in[l].astype(BF16),
                           ffn1_w_out[l].astype(BF16), 0.5, seq)
        mix, gates = _inproj(xt, mod[l, :, 1], norm_g[l, 2:3], w_in[l].astype(BF16), seq)
        mix3 = mix.reshape(bsz, seq, MIX_COLS)
        ya = _hgrn(mix3, lb_all[l:l + 1], hgrn_norm_g[l:l + 1]).reshape(bsz * seq, A_V)
        yb = _stick_breaking(mix3).reshape(bsz * seq, B_W)
        yc = _dilated(mix3).reshape(bsz * seq, C_OUT)
        xt = _mixout(xt, mod[l, :, 1], norm_g[l, 3:4], ya, yb, yc, gates,
                     w_branch_a[l].astype(BF16), w_branch_b[l].astype(BF16), w_branch_c[l].astype(BF16),
                     w_out[l].astype(BF16), seq)
        xt = _ffn_sublayer(xt, mod[l, :, 2], norm_g[l, 4:6], ffn2_w_in[l].astype(BF16),
                           ffn2_w_out[l].astype(BF16), 0.5, seq)
    return xt.reshape(bsz, seq, d)
```

```python
import functools
import math

import jax
import jax.numpy as jnp
from jax import lax
from jax.experimental import pallas as pl
from jax.experimental.pallas import tpu as pltpu

D_MODEL = 1024
DEPTH = 4
A_HEADS = 6
A_KDIM = 128
A_VDIM = 64
A_CHUNK = 64
B_HEADS = 6
B_HDIM = 64
B_BLOCK = 128
C_GROUPS = ((128, 1), (512, 4), (2048, 16))
C_HEADS_PER_GROUP = 4
C_HEADS = C_HEADS_PER_GROUP * len(C_GROUPS)
C_HDIM = 64
C_BLOCK = 128
D_FF = 2816
N_BRANCH = 3
EPS = 1e-6
NEG_BIG = -1e30
TINY = 1e-30

A_QK = A_HEADS * A_KDIM
A_V = A_HEADS * A_VDIM
B_W = B_HEADS * B_HDIM
C_W = C_HEADS * C_HDIM
C_OUT = C_HEADS_PER_GROUP * C_HDIM
MIX_COLS = 2 * A_QK + 2 * A_V + 3 * B_W + 3 * C_W
GATE_COLS = N_BRANCH * D_MODEL
IN_COLS = MIX_COLS + GATE_COLS

LANES = 128
MXU_WIDTH = 256
OFF_AQ = 0
OFF_AF = A_QK // LANES
OFF_AI = 2 * A_QK // LANES
OFF_AG = (2 * A_QK + A_V) // LANES
OFF_BQ = (2 * A_QK + 2 * A_V) // LANES
OFF_BK = OFF_BQ + B_W // LANES
OFF_BV = OFF_BK + B_W // LANES
OFF_CQ = OFF_BV + B_W // LANES
OFF_CK = OFF_CQ + C_W // LANES
OFF_CV = OFF_CK + C_W // LANES
MIX_BLOCKS = MIX_COLS // LANES

VMEM_LIMIT = 56 * 1024 * 1024
SB_SKIP_LOG = -105.0
LOG2_E = 1.4426950408889634
SB_FIXED_BLOCKS = 3
HGRN_BATCH = 16
SB_QBLOCKS_PER_ITER = 8
DIL_BATCH = 4
HGRN_SAFE_SPAN = 80.0

F32 = jnp.float32
BF16 = jnp.bfloat16


def _alibi_slopes(n):
    def pow2_slopes(m):
        start = 2.0 ** (-8.0 / m)
        return [start ** (i + 1) for i in range(m)]
    if math.log2(n).is_integer():
        s = pow2_slopes(n)
    else:
        c = 2 ** int(math.floor(math.log2(n)))
        s = pow2_slopes(c) + pow2_slopes(2 * c)[0::2][: n - c]
    return sorted(s, reverse=True)


def _rms(x, g):
    return x * lax.rsqrt(jnp.mean(x * x, axis=-1, keepdims=True) + EPS) * g


def _sigmoid(x):
    return 1.0 / (1.0 + jnp.exp(-x))


def _dot(a, b):
    return jnp.dot(a, b, preferred_element_type=F32)


def _dot_nt(a, b):
    return lax.dot_general(a, b, (((1,), (1,)), ((), ())), preferred_element_type=F32)


def _split_bf16(x):
    hi = x.astype(BF16)
    lo = (x - hi.astype(F32)).astype(BF16)
    return hi, lo


def _cparams(sem):
    return pltpu.CompilerParams(dimension_semantics=sem, vmem_limit_bytes=VMEM_LIMIT)


def _resident(shape):
    nd = len(shape)
    return pl.BlockSpec(shape, lambda *_: (0,) * nd, pipeline_mode=pl.Buffered(1))


def _ada_kernel(c_ref, w_ref, b_ref, o_ref):
    c = c_ref[...]
    ca = c * _sigmoid(c)
    o_ref[0] = jnp.dot(ca, w_ref[0], precision=lax.Precision.HIGHEST,
                       preferred_element_type=F32) + b_ref[0]


def _ada(c, w_ada, b_ada):
    depth, d, n = w_ada.shape
    bsz = c.shape[0]
    tn = 1152
    return pl.pallas_call(
        _ada_kernel,
        grid=(depth, n // tn),
        in_specs=[pl.BlockSpec((bsz, d), lambda l, j: (0, 0)),
                  pl.BlockSpec((1, d, tn), lambda l, j: (l, 0, j)),
                  pl.BlockSpec((1, 1, tn), lambda l, j: (l, 0, j))],
        out_specs=pl.BlockSpec((1, bsz, tn), lambda l, j: (l, 0, j)),
        out_shape=jax.ShapeDtypeStruct((depth, bsz, n), F32),
        compiler_params=_cparams(("parallel", "parallel")),
        name="ada_mod",
    )(c, w_ada, b_ada.reshape(depth, 1, n))


def _lb_kernel(l_ref, o_ref):
    x = l_ref[...]
    e = jnp.exp(x - jnp.max(x, axis=0, keepdims=True))
    p = e / jnp.sum(e, axis=0, keepdims=True)
    depth = x.shape[0]
    run = p[0:1]
    o_ref[0:1, :] = run - p[0:1]
    for l in range(1, depth):
        run = run + p[l:l + 1]
        o_ref[l:l + 1, :] = run - p[0:1]


def _lower_bounds(logits):
    return pl.pallas_call(_lb_kernel, out_shape=jax.ShapeDtypeStruct(logits.shape, F32),
                          name="hgrn_lower_bounds")(logits)


def _ffn_kernel(x_ref, mod_ref, g_ref, win_ref, wout_ref, o_ref, act_ref, *, res_w, ffc, sub):
    m = mod_ref[0]
    d_ff = wout_ref.shape[0]
    subs = [slice(s * sub, (s + 1) * sub) for s in range(x_ref.shape[0] // sub)]
    hb = [(_rms(x_ref[rs, :], g_ref[0:1]) * (1.0 + m[1:2]) + m[0:1]).astype(BF16) for rs in subs]
    for s, rs in enumerate(subs):
        for c in range(d_ff // ffc):
            a = _dot(hb[s], win_ref[:, c * ffc:(c + 1) * ffc])
            b = _dot(hb[s], win_ref[:, d_ff + c * ffc:d_ff + (c + 1) * ffc])
            act_ref[rs, c * ffc:(c + 1) * ffc] = (a * _sigmoid(a) * b).astype(BF16)
    ys = [_dot(act_ref[rs, :], wout_ref[...]) for rs in subs]
    for s, rs in enumerate(subs):
        o_ref[rs, :] = x_ref[rs, :] + res_w * m[2:3] * _rms(ys[s], g_ref[1:2])


def _ffn_sublayer(x, mod, g2, w_in, w_out, res_w, seq, tm=1024, ffc=MXU_WIDTH, sub=512):
    t, d = x.shape
    d_ff = w_out.shape[0]
    per_b = seq // tm
    return pl.pallas_call(
        functools.partial(_ffn_kernel, res_w=res_w, ffc=ffc, sub=sub),
        grid=(t // tm,),
        in_specs=[pl.BlockSpec((tm, d), lambda i: (i, 0)),
                  pl.BlockSpec((1, 3, d), lambda i: (i // per_b, 0, 0)),
                  _resident((2, d)),
                  _resident((d, 2 * d_ff)),
                  _resident((d_ff, d))],
        out_specs=pl.BlockSpec((tm, d), lambda i: (i, 0)),
        out_shape=jax.ShapeDtypeStruct((t, d), F32),
        scratch_shapes=[pltpu.VMEM((tm, d_ff), BF16)],
        compiler_params=_cparams(("parallel",)),
        name="ffn_sublayer",
    )(x, mod, g2, w_in, w_out)


def _inproj_kernel(x_ref, mod_ref, g_ref, w_ref, mix_ref, gate_ref, *, cc, sub):
    m = mod_ref[0]
    subs = [slice(s * sub, (s + 1) * sub) for s in range(x_ref.shape[0] // sub)]
    hb = [(_rms(x_ref[rs, :], g_ref[...]) * (1.0 + m[1:2]) + m[0:1]).astype(BF16) for rs in subs]
    for s, rs in enumerate(subs):
        for lo in range(0, IN_COLS, cc):
            hi = min(lo + cc, IN_COLS)
            z = _dot(hb[s], w_ref[:, lo:hi])
            mix_hi = min(hi, MIX_COLS)
            if lo < mix_hi:
                mix_ref[rs, lo:mix_hi] = z[:, :mix_hi - lo]
            if hi > MIX_COLS:
                g_lo = max(lo, MIX_COLS)
                gate_ref[rs, g_lo - MIX_COLS:hi - MIX_COLS] = _sigmoid(z[:, g_lo - lo:]).astype(BF16)


def _inproj(x, mod, g, w_in, seq, tm=512, cc=MXU_WIDTH, sub=256):
    t, d = x.shape
    per_b = seq // tm
    return pl.pallas_call(
        functools.partial(_inproj_kernel, cc=cc, sub=sub),
        grid=(t // tm,),
        in_specs=[pl.BlockSpec((tm, d), lambda i: (i, 0)),
                  pl.BlockSpec((1, 3, d), lambda i: (i // per_b, 0, 0)),
                  _resident((1, d)),
                  _resident((d, IN_COLS))],
        out_specs=[pl.BlockSpec((tm, MIX_COLS), lambda i: (i, 0)),
                   pl.BlockSpec((tm, GATE_COLS), lambda i: (i, 0))],
        out_shape=[jax.ShapeDtypeStruct((t, MIX_COLS), F32),
                   jax.ShapeDtypeStruct((t, GATE_COLS), BF16)],
        compiler_params=_cparams(("parallel",)),
        name="mix_inproj",
    )(x, mod, g, w_in)


def _hgrn_kernel(q_ref, f_ref, i_ref, g_ref, lb_ref, ng_ref, o_ref,
                 state_ref, qs_ref, ks_ref, bs_ref, oi_ref, *, ts):
    ck, kd, vd = A_CHUNK, A_KDIM, A_VDIM
    nc = ts // ck
    mid = ck // 2 - 1

    @pl.when(pl.program_id(2) == 0)
    def _():
        state_ref[...] = jnp.zeros_like(state_ref)

    def iota(shape, axis):
        return lax.broadcasted_iota(jnp.int32, shape, axis)

    tri2 = (iota((ck, 2 * ck), 0) >= iota((ck, 2 * ck), 1) % ck).astype(BF16)
    row1 = iota((ck, 1), 0)
    ng = ng_ref[...]
    ng2 = jnp.concatenate([ng, ng], axis=1)
    rows_of = lambda c: slice(c * ck, (c + 1) * ck)

    qraw = q_ref[0]
    fr = f_ref[0]
    lb = lb_ref[...]
    qs = qraw * _sigmoid(qraw)
    e = jnp.exp(-jnp.abs(fr))
    big = 1.0 / (1.0 + e)
    small = e * big
    pos = fr >= 0.0
    f = lb + (1.0 - lb) * jnp.where(pos, big, small)
    k = (1.0 - lb) * jnp.where(pos, small, big)
    lf_hi, lf_lo = _split_bf16(jnp.log(jnp.maximum(f, TINY)))
    b = jnp.concatenate(
        [_dot(tri2, jnp.concatenate([lf_hi[rows_of(c)], lf_lo[rows_of(c)]], axis=0)) for c in range(nc)],
        axis=0)
    worst = None
    for c in range(nc):
        b_mid = b[c * ck + mid:c * ck + mid + 1]
        b_end = b[(c + 1) * ck - 1:(c + 1) * ck]
        w = jnp.minimum(b_mid, b_end - b_mid)
        worst = w if worst is None else jnp.minimum(worst, w)
    safe = jnp.min(worst) > -HGRN_SAFE_SPAN

    @pl.when(safe)
    def _():
        key_head0 = iota((ck, 2 * kd), 1) < kd
        val_head0 = iota((ck, 2 * vd), 1) < vd
        causal2 = iota((ck, 2 * ck), 0) >= iota((ck, 2 * ck), 1) % ck
        r4 = iota((4 * vd, 2 * vd), 0) % (2 * vd)
        same_head_sum = ((r4 < vd) == (iota((4 * vd, 2 * vd), 1) < vd)).astype(BF16)

        def block_diag(x, head0):
            zero = jnp.zeros_like(x)
            return jnp.concatenate([jnp.where(head0, x, zero), jnp.where(head0, zero, x)], axis=0)

        sts = [state_ref[0:vd, 0:kd], state_ref[vd:, kd:]]
        zero_block = jnp.zeros((vd, kd), BF16)
        for first in range(0, nc, HGRN_BATCH):
            cs = range(first, min(first + HGRN_BATCH, nc))
            qa, ka, q_dec, k_end, end_decay, vb = {}, {}, {}, {}, {}, {}
            for c in cs:
                bc = b[rows_of(c)]
                b_mid = bc[mid:mid + 1]
                b_end = bc[ck - 1:ck]
                qa_f = qs[rows_of(c)] * jnp.exp(bc - b_mid)
                ka_f = k[rows_of(c)] * jnp.exp(b_mid - bc)
                qa[c] = qa_f.astype(BF16)
                ka[c] = block_diag(ka_f.astype(BF16), key_head0)
                q_dec[c] = (qa_f * jnp.exp(b_mid)).astype(BF16)
                k_end[c] = (ka_f * jnp.exp(b_end - b_mid)).astype(BF16)
                end_decay[c] = jnp.exp(b_end)
                vb[c] = i_ref[0, rows_of(c), :].astype(BF16)
            scores = {c: jnp.where(causal2, _dot_nt(qa[c], ka[c]), 0.0).astype(BF16) for c in cs}
            o_intra = {c: _dot(scores[c], block_diag(vb[c], val_head0)) for c in cs}
            update = {c: lax.dot_general(vb[c], k_end[c], (((0,), (0,)), ((), ())), preferred_element_type=F32)
                      for c in cs}
            state_before = {}
            for c in cs:
                state_before[c] = jnp.concatenate(
                    [jnp.concatenate([sts[0].astype(BF16), zero_block], axis=1),
                     jnp.concatenate([zero_block, sts[1].astype(BF16)], axis=1)], axis=0)
                sts = [end_decay[c][:, 0:kd] * sts[0] + update[c][0:vd, 0:kd],
                       end_decay[c][:, kd:] * sts[1] + update[c][vd:, kd:]]
            outs = {c: o_intra[c] + _dot_nt(q_dec[c], state_before[c]) for c in cs}
            squares = {}
            for c in cs:
                hi, lo = _split_bf16(outs[c] * outs[c])
                squares[c] = _dot(jnp.concatenate([hi, lo], axis=1), same_head_sum)
            for c in cs:
                gate = g_ref[0, rows_of(c), :]
                y = outs[c] * lax.rsqrt(squares[c] * (1.0 / vd) + EPS) * ng2
                o_ref[0, rows_of(c), :] = (y * (gate * _sigmoid(gate))).astype(o_ref.dtype)
        state_ref[0:vd, 0:kd] = sts[0]
        state_ref[vd:, kd:] = sts[1]

    @pl.when(jnp.logical_not(safe))
    def _():
        for j in range(2):
            ksl = slice(j * kd, (j + 1) * kd)
            vsl = slice(j * vd, (j + 1) * vd)
            qs_ref[j] = qs[:, ksl]
            ks_ref[j] = k[:, ksl]
            bs_ref[j] = b[:, ksl]

            def chunk(c, carry):
                r0 = pl.multiple_of(c * ck, ck)
                rows = pl.ds(r0, ck)
                bc = bs_ref[j, rows, :]
                qc = qs_ref[j, rows, :]
                kc = ks_ref[j, rows, :]
                v = i_ref[0, rows, vsl]
                vb = v.astype(BF16)

                def one_row(t, cr):
                    bt = bs_ref[j, pl.ds(r0 + t, 1), :]
                    qt = qs_ref[j, pl.ds(r0 + t, 1), :]
                    dec = jnp.exp(jnp.where(row1 <= t, bt - bc, NEG_BIG))
                    sc = jnp.sum(dec * qt * kc, axis=-1, keepdims=True)
                    oi_ref[pl.ds(t, 1), :] = jnp.sum(sc * v, axis=0, keepdims=True)
                    return cr

                lax.fori_loop(0, ck, one_row, 0)
                b_end = bc[ck - 1:ck]
                st = state_ref[vsl, ksl]
                o = oi_ref[...] + _dot_nt((qc * jnp.exp(bc)).astype(BF16), st.astype(BF16))
                k_end = (kc * jnp.exp(b_end - bc)).astype(BF16)
                state_ref[vsl, ksl] = jnp.exp(b_end) * st + lax.dot_general(
                    vb, k_end, (((0,), (0,)), ((), ())), preferred_element_type=F32)
                gate = g_ref[0, rows, vsl]
                o_ref[0, rows, vsl] = (_rms(o, ng) * (gate * _sigmoid(gate))).astype(o_ref.dtype)
                return carry

            lax.fori_loop(0, nc, chunk, 0)


def _hgrn(mix3, lb, ng, ts=1024):
    bsz, seq, _ = mix3.shape
    pairs = A_HEADS // 2
    kw, vw = 2 * A_KDIM, 2 * A_VDIM
    return pl.pallas_call(
        functools.partial(_hgrn_kernel, ts=ts),
        grid=(bsz, pairs, seq // ts),
        in_specs=[pl.BlockSpec((1, ts, kw), lambda b, p, s: (b, s, OFF_AQ * LANES // kw + p)),
                  pl.BlockSpec((1, ts, kw), lambda b, p, s: (b, s, OFF_AF * LANES // kw + p)),
                  pl.BlockSpec((1, ts, vw), lambda b, p, s: (b, s, OFF_AI + p)),
                  pl.BlockSpec((1, ts, vw), lambda b, p, s: (b, s, OFF_AG + p)),
                  pl.BlockSpec((1, kw), lambda b, p, s: (0, p)),
                  pl.BlockSpec((1, A_VDIM), lambda b, p, s: (0, 0))],
        out_specs=pl.BlockSpec((1, ts, vw), lambda b, p, s: (b, s, p)),
        out_shape=jax.ShapeDtypeStruct((bsz, seq, A_V), BF16),
        scratch_shapes=[pltpu.VMEM((2 * A_VDIM, 2 * A_KDIM), F32),
                        pltpu.VMEM((2, ts, A_KDIM), F32),
                        pltpu.VMEM((2, ts, A_KDIM), F32),
                        pltpu.VMEM((2, ts, A_KDIM), F32),
                        pltpu.VMEM((A_CHUNK, A_VDIM), F32)],
        compiler_params=_cparams(("parallel", "parallel", "arbitrary")),
        name="hgrn2",
    )(mix3, mix3, mix3, mix3, lb, ng)


def _sb_kernel(q_ref, k_ref, v_ref, o_ref, run_ref, acc_ref):
    blk, d = B_BLOCK, B_HDIM
    seq = q_ref.shape[1]
    neg_scale = -(d ** -0.5) * LOG2_E
    row = lax.broadcasted_iota(jnp.int32, (blk, 2 * blk), 0)
    col = lax.broadcasted_iota(jnp.int32, (blk, 2 * blk), 1)
    key_in_block = jnp.where(col >= blk, col - blk, col)
    strictly_before = key_in_block < row
    r2 = lax.broadcasted_iota(jnp.int32, (2 * blk, 2 * blk), 0)
    c2 = lax.broadcasted_iota(jnp.int32, (2 * blk, 2 * blk), 1)
    j2 = jnp.where(r2 >= blk, r2 - blk, r2)
    later_and_ones = jnp.logical_or(j2 > c2, c2 >= blk).astype(BF16)
    head0_lanes = lax.broadcasted_iota(jnp.int32, (blk, 2 * d), 1) < d

    def load_neg_q(q0):
        return (q_ref[0, pl.ds(q0, blk), :] * neg_scale).astype(BF16)

    def block_diag(x):
        zero = jnp.zeros_like(x)
        return jnp.concatenate([jnp.where(head0_lanes, x, zero), jnp.where(head0_lanes, zero, x)], axis=0)

    def logs(neg_z):
        log_1m = jnp.minimum(neg_z, 0.0) - jnp.log(1.0 + jnp.exp2(-jnp.abs(neg_z))) * LOG2_E
        return log_1m, log_1m - neg_z

    def later_sums(log_1m):
        hi, lo = _split_bf16(log_1m)
        per_head = [_dot(jnp.concatenate([hi[:, h * blk:(h + 1) * blk], lo[:, h * blk:(h + 1) * blk]], axis=1),
                         later_and_ones) for h in range(2)]
        within = jnp.concatenate([per_head[0][:, :blk], per_head[1][:, :blk]], axis=1)
        total = jnp.concatenate([per_head[0][:, blk:], per_head[1][:, blk:]], axis=1)
        return within, total

    def qgroup(qg, carry0, edge):
        qbs = [qg * SB_QBLOCKS_PER_ITER + i for i in range(SB_QBLOCKS_PER_ITER)]
        q0s = [pl.multiple_of(qb * blk, blk) for qb in qbs]
        units = [(i, step) for i in range(SB_QBLOCKS_PER_ITER) for step in range(SB_FIXED_BLOCKS)]
        keeps, vals, neg_z = {}, {}, {}
        for i, step in units:
            kb = qbs[i] - step
            k0 = pl.multiple_of(jnp.maximum(kb, 0) * blk, blk)
            keeps[i, step] = strictly_before if step == 0 else (kb >= 0 if edge else None)
            vals[i, step] = block_diag(v_ref[0, pl.ds(k0, blk), :].astype(BF16))
            neg_z[i, step] = _dot_nt(load_neg_q(q0s[i]), block_diag(k_ref[0, pl.ds(k0, blk), :].astype(BF16)))
        log_bs, sums = {}, {}
        for u in units:
            log_1m, log_bs[u] = logs(neg_z[u])
            sums[u] = log_1m if keeps[u] is None else jnp.where(keeps[u], log_1m, 0.0)
        for u in units:
            sums[u] = later_sums(sums[u])
        weights = {}
        tops = {}
        for i in range(SB_QBLOCKS_PER_ITER):
            run = jnp.zeros((blk, 2 * blk), F32)
            for step in range(SB_FIXED_BLOCKS):
                u = (i, step)
                within, total = sums[u]
                a = jnp.exp2(log_bs[u] + within + run)
                weights[u] = (a if keeps[u] is None else jnp.where(keeps[u], a, 0.0)).astype(BF16)
                run = run + total
            run_ref[i] = run
            tops[i] = jnp.max(run)
        for i in range(SB_QBLOCKS_PER_ITER):
            acc = _dot(weights[i, 0], vals[i, 0])
            for step in range(1, SB_FIXED_BLOCKS):
                acc = acc + _dot(weights[i, step], vals[i, step])
            acc_ref[i] = acc

        def cond(st):
            kb, top = st
            return jnp.logical_and(kb >= 0, top > SB_SKIP_LOG * LOG2_E)

        for i in range(SB_QBLOCKS_PER_ITER):
            def body(st, i=i):
                kb, _ = st
                k0 = pl.multiple_of(kb * blk, blk)
                log_1m, log_b = logs(_dot_nt(load_neg_q(q0s[i]),
                                             block_diag(k_ref[0, pl.ds(k0, blk), :].astype(BF16))))
                within, total = later_sums(log_1m)
                run = run_ref[i]
                a = jnp.exp2(log_b + within + run)
                acc_ref[i] = acc_ref[i] + _dot(a.astype(BF16), block_diag(v_ref[0, pl.ds(k0, blk), :].astype(BF16)))
                run = run + total
                run_ref[i] = run
                return kb - 1, jnp.max(run)

            lax.while_loop(cond, body, (qbs[i] - SB_FIXED_BLOCKS, tops[i]))
            o_ref[0, pl.ds(q0s[i], blk), :] = acc_ref[i].astype(o_ref.dtype)
        return carry0

    first_full = -(-(SB_FIXED_BLOCKS - 1) // SB_QBLOCKS_PER_ITER)
    for qg in range(first_full):
        qgroup(qg, 0, True)
    lax.fori_loop(first_full, seq // (blk * SB_QBLOCKS_PER_ITER), functools.partial(qgroup, edge=False), 0)


def _stick_breaking(mix3):
    bsz, seq, _ = mix3.shape
    pairs = B_HEADS // 2
    w = 2 * B_HDIM
    return pl.pallas_call(
        _sb_kernel,
        grid=(bsz, pairs),
        in_specs=[pl.BlockSpec((1, seq, w), lambda b, p: (b, 0, OFF_BQ + p)),
                  pl.BlockSpec((1, seq, w), lambda b, p: (b, 0, OFF_BK + p)),
                  pl.BlockSpec((1, seq, w), lambda b, p: (b, 0, OFF_BV + p))],
        out_specs=pl.BlockSpec((1, seq, w), lambda b, p: (b, 0, p)),
        out_shape=jax.ShapeDtypeStruct((bsz, seq, B_W), BF16),
        scratch_shapes=[pltpu.VMEM((SB_QBLOCKS_PER_ITER, B_BLOCK, 2 * B_BLOCK), F32),
                        pltpu.VMEM((SB_QBLOCKS_PER_ITER, B_BLOCK, 2 * B_HDIM), F32)],
        compiler_params=_cparams(("parallel", "parallel")),
        name="stick_breaking",
    )(mix3, mix3, mix3)


def _dil_kernel(*refs, tile, slopes):
    blk, d = C_BLOCK, C_HDIM
    n_groups = len(C_GROUPS)
    o_ref, os_ref, ls_ref = refs[5 * n_groups:]
    pair = pl.program_id(1)
    first_tile = pl.program_id(2) == 0
    scale = d ** -0.5
    qi = lax.broadcasted_iota(jnp.int32, (blk, 4 * blk), 0)
    lane4 = lax.broadcasted_iota(jnp.int32, (blk, 4 * blk), 1)
    kj = lane4 % (2 * blk)
    head0_keys = lane4 < 2 * blk
    delta = qi + blk - kj
    head0_lanes = lax.broadcasted_iota(jnp.int32, (2 * blk, 2 * d), 1) < d
    head0_out = lax.broadcasted_iota(jnp.int32, (blk, 2 * d), 1) < d
    ones0 = head0_lanes.astype(BF16)
    ones1 = jnp.logical_not(head0_lanes).astype(BF16)

    def block_diag_keys(k2):
        zero = jnp.zeros_like(k2)
        return jnp.concatenate([jnp.where(head0_lanes, k2, zero), jnp.where(head0_lanes, zero, k2)], axis=0)

    def block_diag_values(v2):
        zero = jnp.zeros_like(v2)
        return jnp.concatenate([jnp.concatenate([jnp.where(head0_lanes, v2, zero), ones0], axis=1),
                                jnp.concatenate([jnp.where(head0_lanes, zero, v2), ones1], axis=1)], axis=0)

    for g, (window, dil) in enumerate(C_GROUPS):
        q_ref, k_ref, kp_ref, v_ref, vp_ref = refs[5 * g:5 * g + 5]
        window_ok = jnp.logical_and(delta >= 0, delta <= window // dil)
        window_ok_first = jnp.logical_and(window_ok, jnp.logical_or(kj >= blk, jnp.logical_not(first_tile)))
        slope0 = jnp.where(pair == 0, jnp.float32(slopes[g][0]), jnp.float32(slopes[g][2]))
        slope1 = jnp.where(pair == 0, jnp.float32(slopes[g][1]), jnp.float32(slopes[g][3]))
        alibi = jnp.where(head0_keys, slope0, slope1) * (delta * dil).astype(F32)

        def strided(ref, rho, block):
            start = rho + block * blk * dil
            return ref[0, pl.ds(start, blk, stride=dil), :] if dil > 1 else ref[0, pl.ds(start, blk), :]

        blocks = [(rho, i) for rho in range(dil) for i in range(tile // (dil * blk))]
        for at in range(0, len(blocks), DIL_BATCH):
            batch = blocks[at:at + DIL_BATCH]
            scores, values = {}, {}
            for rho, i in batch:
                k2 = jnp.concatenate([strided(kp_ref, rho, 0) if i == 0 else strided(k_ref, rho, i - 1),
                                      strided(k_ref, rho, i)], axis=0).astype(BF16)
                v2 = jnp.concatenate([strided(vp_ref, rho, 0) if i == 0 else strided(v_ref, rho, i - 1),
                                      strided(v_ref, rho, i)], axis=0).astype(BF16)
                scores[rho, i] = _dot_nt((strided(q_ref, rho, i) * scale).astype(BF16), block_diag_keys(k2))
                values[rho, i] = block_diag_values(v2)
            probs, tops = {}, {}
            for rho, i in batch:
                valid = window_ok_first if i == 0 else window_ok
                s = jnp.where(valid, scores[rho, i] - alibi, NEG_BIG)
                mx0 = jnp.max(s[:, :2 * blk], axis=-1, keepdims=True)
                mx1 = jnp.max(s[:, 2 * blk:], axis=-1, keepdims=True)
                probs[rho, i] = jnp.exp(s - jnp.where(head0_keys, mx0, mx1)).astype(BF16)
                tops[rho, i] = jnp.where(head0_out, mx0, mx1)
            for rho, i in batch:
                r = _dot(probs[rho, i], values[rho, i])
                den = r[:, 2 * d:]
                start = rho + i * blk * dil
                rows = pl.ds(start, blk, stride=dil) if dil > 1 else pl.ds(start, blk)
                os_ref[g, rows, :] = r[:, :2 * d] / den
                ls_ref[g, rows, :] = tops[rho, i] + jnp.log(den)
    lses = [ls_ref[g] for g in range(n_groups)]
    top = functools.reduce(jnp.maximum, lses)
    wts = [jnp.exp(l - top) for l in lses]
    num = functools.reduce(lambda a, b: a + b, [wts[g] * os_ref[g] for g in range(n_groups)])
    o_ref[0] = (num / functools.reduce(lambda a, b: a + b, wts)).astype(o_ref.dtype)


def _dilated(mix3):
    bsz, seq, _ = mix3.shape
    pairs = C_HEADS_PER_GROUP // 2
    w = 2 * C_HDIM
    tile = C_BLOCK * max(dil for _, dil in C_GROUPS)
    all_slopes = _alibi_slopes(C_HEADS)
    slopes = tuple(tuple(all_slopes[g * C_HEADS_PER_GROUP:(g + 1) * C_HEADS_PER_GROUP])
                   for g in range(len(C_GROUPS)))
    in_specs = []
    for g, (_, dil) in enumerate(C_GROUPS):
        back = C_BLOCK * dil
        per = tile // back

        def cur(off, g=g):
            return pl.BlockSpec((1, tile, w), lambda b, p, n: (b, n, off + pairs * g + p))

        def prev(off, g=g, back=back, per=per):
            return pl.BlockSpec((1, back, w), lambda b, p, n: (b, jnp.maximum(n * per - 1, 0), off + pairs * g + p))

        in_specs += [cur(OFF_CQ), cur(OFF_CK), prev(OFF_CK), cur(OFF_CV), prev(OFF_CV)]
    return pl.pallas_call(
        functools.partial(_dil_kernel, tile=tile, slopes=slopes),
        grid=(bsz, pairs, seq // tile),
        in_specs=in_specs,
        out_specs=pl.BlockSpec((1, tile, w), lambda b, p, n: (b, n, p)),
        out_shape=jax.ShapeDtypeStruct((bsz, seq, C_OUT), BF16),
        scratch_shapes=[pltpu.VMEM((len(C_GROUPS), tile, w), F32),
                        pltpu.VMEM((len(C_GROUPS), tile, w), F32)],
        compiler_params=_cparams(("parallel", "parallel", "arbitrary")),
        name="dilated",
    )(*([mix3] * (5 * len(C_GROUPS))))


def _mixout_kernel(x_ref, mod_ref, g_ref, ya_ref, yb_ref, yc_ref, gate_ref, wa_ref, wb_ref, wc_ref, wo_ref,
                   out_ref, *, sub):
    d = D_MODEL
    subs = [slice(s * sub, (s + 1) * sub) for s in range(x_ref.shape[0] // sub)]
    merged = []
    for rs in subs:
        m = gate_ref[rs, 0:d].astype(F32) * _dot(ya_ref[rs, :], wa_ref[...])
        m = m + gate_ref[rs, d:2 * d].astype(F32) * _dot(yb_ref[rs, :], wb_ref[...])
        m = m + gate_ref[rs, 2 * d:3 * d].astype(F32) * _dot(yc_ref[rs, :], wc_ref[...])
        merged.append(m.astype(BF16))
    ys = [_dot(m, wo_ref[...]) for m in merged]
    for rs, y in zip(subs, ys):
        out_ref[rs, :] = x_ref[rs, :] + mod_ref[0][2:3] * _rms(y, g_ref[...])


def _mixout(x, mod, g, ya, yb, yc, gates, wa, wb, wc, wo, seq, tm=1024, sub=512):
    t, d = x.shape
    per_b = seq // tm

    def tok(wd):
        return pl.BlockSpec((tm, wd), lambda i: (i, 0))

    return pl.pallas_call(
        functools.partial(_mixout_kernel, sub=sub),
        grid=(t // tm,),
        in_specs=[tok(d), pl.BlockSpec((1, 3, d), lambda i: (i // per_b, 0, 0)), _resident((1, d)),
                  tok(A_V), tok(B_W), tok(C_OUT), tok(GATE_COLS),
                  _resident(wa.shape), _resident(wb.shape), _resident(wc.shape), _resident(wo.shape)],
        out_specs=tok(d),
        out_shape=jax.ShapeDtypeStruct((t, d), F32),
        compiler_params=_cparams(("parallel",)),
        name="mix_out",
    )(x, mod, g, ya, yb, yc, gates, wa, wb, wc, wo)


def kernel(x, c, w_ada, b_ada, norm_g, ffn1_w_in, ffn1_w_out, w_in, hgrn_lb_logits, hgrn_norm_g,
           w_branch_a, w_branch_b, w_branch_c, w_out, ffn2_w_in, ffn2_w_out):
    bsz, seq, d = x.shape
    depth = w_ada.shape[0]
    lb_all = _lower_bounds(hgrn_lb_logits.astype(F32))
    mod = _ada(c, w_ada, b_ada).reshape(depth, bsz, 3, 3, d)
    xt = x.reshape(bsz * seq, d)
    for l in range(depth):
        xt = _ffn_sublayer(xt, mod[l, :, 0], norm_g[l, 0:2], ffn1_w_in[l].astype(BF16),
                           ffn1_w_out[l].astype(BF16), 0.5, seq)
        mix, gates = _inproj(xt, mod[l, :, 1], norm_g[l, 2:3], w_in[l].astype(BF16), seq)
        mix3 = mix.reshape(bsz, seq, MIX_COLS)
        ya = _hgrn(mix3, lb_all[l:l + 1], hgrn_norm_g[l:l + 1]).reshape(bsz * seq, A_V)
        yb = _stick_breaking(mix3).reshape(bsz * seq, B_W)
        yc = _dilated(mix3).reshape(bsz * seq, C_OUT)
        xt = _mixout(xt, mod[l, :, 1], norm_g[l, 3:4], ya, yb, yc, gates,
                     w_branch_a[l].astype(BF16), w_branch_b[l].astype(BF16), w_branch_c[l].astype(BF16),
                     w_out[l].astype(BF16), seq)
        xt = _ffn_sublayer(xt, mod[l, :, 2], norm_g[l, 4:6], ffn2_w_in[l].astype(BF16),
                           ffn2_w_out[l].astype(BF16), 0.5, seq)
    return xt.reshape(bsz, seq, d)
```

```python
import functools
import math

import jax
import jax.numpy as jnp
from jax import lax
from jax.experimental import pallas as pl
from jax.experimental.pallas import tpu as pltpu

D_MODEL = 1024
DEPTH = 4
A_HEADS = 6
A_KDIM = 128
A_VDIM = 64
A_CHUNK = 64
B_HEADS = 6
B_HDIM = 64
B_BLOCK = 128
C_GROUPS = ((128, 1), (512, 4), (2048, 16))
C_HEADS_PER_GROUP = 4
C_HEADS = C_HEADS_PER_GROUP * len(C_GROUPS)
C_HDIM = 64
C_BLOCK = 128
D_FF = 2816
N_BRANCH = 3
EPS = 1e-6
NEG_BIG = -1e30
TINY = 1e-30

A_QK = A_HEADS * A_KDIM
A_V = A_HEADS * A_VDIM
B_W = B_HEADS * B_HDIM
C_W = C_HEADS * C_HDIM
C_OUT = C_HEADS_PER_GROUP * C_HDIM
MIX_COLS = 2 * A_QK + 2 * A_V + 3 * B_W + 3 * C_W
GATE_COLS = N_BRANCH * D_MODEL
IN_COLS = MIX_COLS + GATE_COLS

LANES = 128
MXU_WIDTH = 256
OFF_AQ = 0
OFF_AF = A_QK // LANES
OFF_AI = 2 * A_QK // LANES
OFF_AG = (2 * A_QK + A_V) // LANES
OFF_BQ = (2 * A_QK + 2 * A_V) // LANES
OFF_BK = OFF_BQ + B_W // LANES
OFF_BV = OFF_BK + B_W // LANES
OFF_CQ = OFF_BV + B_W // LANES
OFF_CK = OFF_CQ + C_W // LANES
OFF_CV = OFF_CK + C_W // LANES
MIX_BLOCKS = MIX_COLS // LANES

VMEM_LIMIT = 56 * 1024 * 1024
SB_SKIP_LOG = -105.0
LOG2_E = 1.4426950408889634
SB_FULL_BLOCKS = 2
HGRN_BATCH = 32
SB_QBLOCKS_PER_ITER = 8
DIL_BATCH = 4
HGRN_SAFE_SPAN = 80.0

F32 = jnp.float32
BF16 = jnp.bfloat16


def _alibi_slopes(n):
    def pow2_slopes(m):
        start = 2.0 ** (-8.0 / m)
        return [start ** (i + 1) for i in range(m)]
    if math.log2(n).is_integer():
        s = pow2_slopes(n)
    else:
        c = 2 ** int(math.floor(math.log2(n)))
        s = pow2_slopes(c) + pow2_slopes(2 * c)[0::2][: n - c]
    return sorted(s, reverse=True)


def _rms(x, g):
    return x * lax.rsqrt(jnp.mean(x * x, axis=-1, keepdims=True) + EPS) * g


def _sigmoid(x):
    return 1.0 / (1.0 + jnp.exp(-x))


def _dot(a, b):
    return jnp.dot(a, b, preferred_element_type=F32)


def _dot_nt(a, b):
    return lax.dot_general(a, b, (((1,), (1,)), ((), ())), preferred_element_type=F32)


def _split_bf16(x):
    hi = x.astype(BF16)
    lo = (x - hi.astype(F32)).astype(BF16)
    return hi, lo


def _cparams(sem):
    return pltpu.CompilerParams(dimension_semantics=sem, vmem_limit_bytes=VMEM_LIMIT)


def _resident(shape):
    nd = len(shape)
    return pl.BlockSpec(shape, lambda *_: (0,) * nd, pipeline_mode=pl.Buffered(1))


def _ada_kernel(c_ref, w_ref, b_ref, o_ref):
    c = c_ref[...]
    ca = c * _sigmoid(c)
    o_ref[0] = jnp.dot(ca, w_ref[0], precision=lax.Precision.HIGHEST,
                       preferred_element_type=F32) + b_ref[0]


def _ada(c, w_ada, b_ada):
    depth, d, n = w_ada.shape
    bsz = c.shape[0]
    tn = 1152
    return pl.pallas_call(
        _ada_kernel,
        grid=(depth, n // tn),
        in_specs=[pl.BlockSpec((bsz, d), lambda l, j: (0, 0)),
                  pl.BlockSpec((1, d, tn), lambda l, j: (l, 0, j)),
                  pl.BlockSpec((1, 1, tn), lambda l, j: (l, 0, j))],
        out_specs=pl.BlockSpec((1, bsz, tn), lambda l, j: (l, 0, j)),
        out_shape=jax.ShapeDtypeStruct((depth, bsz, n), F32),
        compiler_params=_cparams(("parallel", "parallel")),
        name="ada_mod",
    )(c, w_ada, b_ada.reshape(depth, 1, n))


def _lb_kernel(l_ref, o_ref):
    x = l_ref[...]
    e = jnp.exp(x - jnp.max(x, axis=0, keepdims=True))
    p = e / jnp.sum(e, axis=0, keepdims=True)
    depth = x.shape[0]
    run = p[0:1]
    o_ref[0:1, :] = run - p[0:1]
    for l in range(1, depth):
        run = run + p[l:l + 1]
        o_ref[l:l + 1, :] = run - p[0:1]


def _lower_bounds(logits):
    return pl.pallas_call(_lb_kernel, out_shape=jax.ShapeDtypeStruct(logits.shape, F32),
                          name="hgrn_lower_bounds")(logits)


def _ffn_kernel(x_ref, mod_ref, g_ref, win_ref, wout_ref, o_ref, act_ref, *, res_w, ffc, sub):
    m = mod_ref[0]
    d_ff = wout_ref.shape[0]
    subs = [slice(s * sub, (s + 1) * sub) for s in range(x_ref.shape[0] // sub)]
    hb = [(_rms(x_ref[rs, :], g_ref[0:1]) * (1.0 + m[1:2]) + m[0:1]).astype(BF16) for rs in subs]
    for s, rs in enumerate(subs):
        for c in range(d_ff // ffc):
            a = _dot(hb[s], win_ref[:, c * ffc:(c + 1) * ffc])
            b = _dot(hb[s], win_ref[:, d_ff + c * ffc:d_ff + (c + 1) * ffc])
            act_ref[rs, c * ffc:(c + 1) * ffc] = (a * _sigmoid(a) * b).astype(BF16)
    ys = [_dot(act_ref[rs, :], wout_ref[...]) for rs in subs]
    for s, rs in enumerate(subs):
        o_ref[rs, :] = x_ref[rs, :] + res_w * m[2:3] * _rms(ys[s], g_ref[1:2])


def _ffn_sublayer(x, mod, g2, w_in, w_out, res_w, seq, tm=1024, ffc=MXU_WIDTH, sub=512):
    t, d = x.shape
    d_ff = w_out.shape[0]
    per_b = seq // tm
    return pl.pallas_call(
        functools.partial(_ffn_kernel, res_w=res_w, ffc=ffc, sub=sub),
        grid=(t // tm,),
        in_specs=[pl.BlockSpec((tm, d), lambda i: (i, 0)),
                  pl.BlockSpec((1, 3, d), lambda i: (i // per_b, 0, 0)),
                  _resident((2, d)),
                  _resident((d, 2 * d_ff)),
                  _resident((d_ff, d))],
        out_specs=pl.BlockSpec((tm, d), lambda i: (i, 0)),
        out_shape=jax.ShapeDtypeStruct((t, d), F32),
        scratch_shapes=[pltpu.VMEM((tm, d_ff), BF16)],
        compiler_params=_cparams(("parallel",)),
        name="ffn_sublayer",
    )(x, mod, g2, w_in, w_out)


def _inproj_kernel(x_ref, mod_ref, g_ref, w_ref, mix_ref, gate_ref, *, cc, sub):
    m = mod_ref[0]
    subs = [slice(s * sub, (s + 1) * sub) for s in range(x_ref.shape[0] // sub)]
    hb = [(_rms(x_ref[rs, :], g_ref[...]) * (1.0 + m[1:2]) + m[0:1]).astype(BF16) for rs in subs]
    for s, rs in enumerate(subs):
        for lo in range(0, IN_COLS, cc):
            hi = min(lo + cc, IN_COLS)
            z = _dot(hb[s], w_ref[:, lo:hi])
            mix_hi = min(hi, MIX_COLS)
            if lo < mix_hi:
                mix_ref[rs, lo:mix_hi] = z[:, :mix_hi - lo]
            if hi > MIX_COLS:
                g_lo = max(lo, MIX_COLS)
                gate_ref[rs, g_lo - MIX_COLS:hi - MIX_COLS] = _sigmoid(z[:, g_lo - lo:]).astype(BF16)


def _inproj(x, mod, g, w_in, seq, tm=512, cc=MXU_WIDTH, sub=256):
    t, d = x.shape
    per_b = seq // tm
    return pl.pallas_call(
        functools.partial(_inproj_kernel, cc=cc, sub=sub),
        grid=(t // tm,),
        in_specs=[pl.BlockSpec((tm, d), lambda i: (i, 0)),
                  pl.BlockSpec((1, 3, d), lambda i: (i // per_b, 0, 0)),
                  _resident((1, d)),
                  _resident((d, IN_COLS))],
        out_specs=[pl.BlockSpec((tm, MIX_COLS), lambda i: (i, 0)),
                   pl.BlockSpec((tm, GATE_COLS), lambda i: (i, 0))],
        out_shape=[jax.ShapeDtypeStruct((t, MIX_COLS), F32),
                   jax.ShapeDtypeStruct((t, GATE_COLS), BF16)],
        compiler_params=_cparams(("parallel",)),
        name="mix_inproj",
    )(x, mod, g, w_in)


def _hgrn_kernel(q_ref, f_ref, i_ref, g_ref, lb_ref, ng_ref, o_ref,
                 state_ref, qs_ref, ks_ref, bs_ref, oi_ref, *, ts):
    ck, kd, vd = A_CHUNK, A_KDIM, A_VDIM
    nc = ts // ck
    mid = ck // 2 - 1

    @pl.when(pl.program_id(2) == 0)
    def _():
        state_ref[...] = jnp.zeros_like(state_ref)

    def iota(shape, axis):
        return lax.broadcasted_iota(jnp.int32, shape, axis)

    tri2 = (iota((ck, 2 * ck), 0) >= iota((ck, 2 * ck), 1) % ck).astype(BF16)
    row1 = iota((ck, 1), 0)
    ng = ng_ref[...]
    ng2 = jnp.concatenate([ng, ng], axis=1)
    rows_of = lambda c: slice(c * ck, (c + 1) * ck)

    qraw = q_ref[0]
    fr = f_ref[0]
    lb = lb_ref[...]
    qs = qraw * _sigmoid(qraw)
    e = jnp.exp(-jnp.abs(fr))
    big = 1.0 / (1.0 + e)
    small = e * big
    pos = fr >= 0.0
    f = lb + (1.0 - lb) * jnp.where(pos, big, small)
    k = (1.0 - lb) * jnp.where(pos, small, big)
    lf_hi, lf_lo = _split_bf16(jnp.log(jnp.maximum(f, TINY)))
    b = jnp.concatenate(
        [_dot(tri2, jnp.concatenate([lf_hi[rows_of(c)], lf_lo[rows_of(c)]], axis=0)) for c in range(nc)],
        axis=0)
    worst = None
    for c in range(nc):
        b_mid = b[c * ck + mid:c * ck + mid + 1]
        b_end = b[(c + 1) * ck - 1:(c + 1) * ck]
        w = jnp.minimum(b_mid, b_end - b_mid)
        worst = w if worst is None else jnp.minimum(worst, w)
    safe = jnp.min(worst) > -HGRN_SAFE_SPAN

    @pl.when(safe)
    def _():
        key_head0 = iota((ck, 2 * kd), 1) < kd
        val_head0 = iota((ck, 2 * vd), 1) < vd
        causal2 = iota((ck, 2 * ck), 0) >= iota((ck, 2 * ck), 1) % ck
        r4 = iota((4 * vd, 2 * vd), 0) % (2 * vd)
        same_head_sum = ((r4 < vd) == (iota((4 * vd, 2 * vd), 1) < vd)).astype(BF16)

        def block_diag(x, head0):
            zero = jnp.zeros_like(x)
            return jnp.concatenate([jnp.where(head0, x, zero), jnp.where(head0, zero, x)], axis=0)

        sts = [state_ref[0:vd, 0:kd], state_ref[vd:, kd:]]
        zero_block = jnp.zeros((vd, kd), BF16)
        for first in range(0, nc, HGRN_BATCH):
            cs = range(first, min(first + HGRN_BATCH, nc))
            qa, ka, q_dec, k_end, end_decay, vb = {}, {}, {}, {}, {}, {}
            for c in cs:
                bc = b[rows_of(c)]
                b_mid = bc[mid:mid + 1]
                b_end = bc[ck - 1:ck]
                qa_f = qs[rows_of(c)] * jnp.exp(bc - b_mid)
                ka_f = k[rows_of(c)] * jnp.exp(b_mid - bc)
                qa[c] = qa_f.astype(BF16)
                ka[c] = block_diag(ka_f.astype(BF16), key_head0)
                q_dec[c] = (qa_f * jnp.exp(b_mid)).astype(BF16)
                k_end[c] = (ka_f * jnp.exp(b_end - b_mid)).astype(BF16)
                end_decay[c] = jnp.exp(b_end)
                vb[c] = i_ref[0, rows_of(c), :].astype(BF16)
            scores = {c: jnp.where(causal2, _dot_nt(qa[c], ka[c]), 0.0).astype(BF16) for c in cs}
            o_intra = {c: _dot(scores[c], block_diag(vb[c], val_head0)) for c in cs}
            update = {c: lax.dot_general(vb[c], k_end[c], (((0,), (0,)), ((), ())), preferred_element_type=F32)
                      for c in cs}
            state_before = {}
            for c in cs:
                state_before[c] = jnp.concatenate(
                    [jnp.concatenate([sts[0].astype(BF16), zero_block], axis=1),
                     jnp.concatenate([zero_block, sts[1].astype(BF16)], axis=1)], axis=0)
                sts = [end_decay[c][:, 0:kd] * sts[0] + update[c][0:vd, 0:kd],
                       end_decay[c][:, kd:] * sts[1] + update[c][vd:, kd:]]
            outs = {c: o_intra[c] + _dot_nt(q_dec[c], state_before[c]) for c in cs}
            squares = {}
            for c in cs:
                hi, lo = _split_bf16(outs[c] * outs[c])
                squares[c] = _dot(jnp.concatenate([hi, lo], axis=1), same_head_sum)
            for c in cs:
                gate = g_ref[0, rows_of(c), :]
                y = outs[c] * lax.rsqrt(squares[c] * (1.0 / vd) + EPS) * ng2
                o_ref[0, rows_of(c), :] = (y * (gate * _sigmoid(gate))).astype(o_ref.dtype)
        state_ref[0:vd, 0:kd] = sts[0]
        state_ref[vd:, kd:] = sts[1]

    @pl.when(jnp.logical_not(safe))
    def _():
        for j in range(2):
            ksl = slice(j * kd, (j + 1) * kd)
            vsl = slice(j * vd, (j + 1) * vd)
            qs_ref[j] = qs[:, ksl]
            ks_ref[j] = k[:, ksl]
            bs_ref[j] = b[:, ksl]

            def chunk(c, carry):
                r0 = pl.multiple_of(c * ck, ck)
                rows = pl.ds(r0, ck)
                bc = bs_ref[j, rows, :]
                qc = qs_ref[j, rows, :]
                kc = ks_ref[j, rows, :]
                v = i_ref[0, rows, vsl]
                vb = v.astype(BF16)

                def one_row(t, cr):
                    bt = bs_ref[j, pl.ds(r0 + t, 1), :]
                    qt = qs_ref[j, pl.ds(r0 + t, 1), :]
                    dec = jnp.exp(jnp.where(row1 <= t, bt - bc, NEG_BIG))
                    sc = jnp.sum(dec * qt * kc, axis=-1, keepdims=True)
                    oi_ref[pl.ds(t, 1), :] = jnp.sum(sc * v, axis=0, keepdims=True)
                    return cr

                lax.fori_loop(0, ck, one_row, 0)
                b_end = bc[ck - 1:ck]
                st = state_ref[vsl, ksl]
                o = oi_ref[...] + _dot_nt((qc * jnp.exp(bc)).astype(BF16), st.astype(BF16))
                k_end = (kc * jnp.exp(b_end - bc)).astype(BF16)
                state_ref[vsl, ksl] = jnp.exp(b_end) * st + lax.dot_general(
                    vb, k_end, (((0,), (0,)), ((), ())), preferred_element_type=F32)
                gate = g_ref[0, rows, vsl]
                o_ref[0, rows, vsl] = (_rms(o, ng) * (gate * _sigmoid(gate))).astype(o_ref.dtype)
                return carry

            lax.fori_loop(0, nc, chunk, 0)


def _hgrn(mix3, lb, ng, ts=2048):
    bsz, seq, _ = mix3.shape
    pairs = A_HEADS // 2
    kw, vw = 2 * A_KDIM, 2 * A_VDIM
    return pl.pallas_call(
        functools.partial(_hgrn_kernel, ts=ts),
        grid=(bsz, pairs, seq // ts),
        in_specs=[pl.BlockSpec((1, ts, kw), lambda b, p, s: (b, s, OFF_AQ * LANES // kw + p)),
                  pl.BlockSpec((1, ts, kw), lambda b, p, s: (b, s, OFF_AF * LANES // kw + p)),
                  pl.BlockSpec((1, ts, vw), lambda b, p, s: (b, s, OFF_AI + p)),
                  pl.BlockSpec((1, ts, vw), lambda b, p, s: (b, s, OFF_AG + p)),
                  pl.BlockSpec((1, kw), lambda b, p, s: (0, p)),
                  pl.BlockSpec((1, A_VDIM), lambda b, p, s: (0, 0))],
        out_specs=pl.BlockSpec((1, ts, vw), lambda b, p, s: (b, s, p)),
        out_shape=jax.ShapeDtypeStruct((bsz, seq, A_V), BF16),
        scratch_shapes=[pltpu.VMEM((2 * A_VDIM, 2 * A_KDIM), F32),
                        pltpu.VMEM((2, ts, A_KDIM), F32),
                        pltpu.VMEM((2, ts, A_KDIM), F32),
                        pltpu.VMEM((2, ts, A_KDIM), F32),
                        pltpu.VMEM((A_CHUNK, A_VDIM), F32)],
        compiler_params=_cparams(("parallel", "parallel", "arbitrary")),
        name="hgrn2",
    )(mix3, mix3, mix3, mix3, lb, ng)


def _sb_kernel(q_ref, k_ref, v_ref, o_ref, run_ref, acc_ref):
    blk, d = B_BLOCK, B_HDIM
    half = blk // 2
    seq = q_ref.shape[1]
    neg_scale = -(d ** -0.5) * LOG2_E

    def iota(shape, axis):
        return lax.broadcasted_iota(jnp.int32, shape, axis)

    strictly_before = iota((blk, 2 * blk), 1) % blk < iota((blk, 2 * blk), 0)
    r2, c2 = iota((2 * blk, 2 * blk), 0), iota((2 * blk, 2 * blk), 1)
    later_and_ones = jnp.logical_or(r2 % blk > c2, c2 >= blk).astype(BF16)
    same_head = (r2 % blk) // half == (c2 % blk) // half
    later_and_ones_half = jnp.logical_and(
        same_head, jnp.logical_or(r2 % half > c2 % half, c2 >= blk)).astype(BF16)
    head0_lanes = iota((blk, 2 * d), 1) < d
    head0_lanes_half = iota((half, 2 * d), 1) < d
    head0_keys_half = iota((blk, 2 * half), 1) < half

    def load_neg_q(q0):
        return (q_ref[0, pl.ds(q0, blk), :] * neg_scale).astype(BF16)

    def block_diag(x, head0):
        zero = jnp.zeros_like(x)
        return jnp.concatenate([jnp.where(head0, x, zero), jnp.where(head0, zero, x)], axis=0)

    def logs(neg_z):
        log_1m = jnp.minimum(neg_z, 0.0) - jnp.log(1.0 + jnp.exp2(-jnp.abs(neg_z))) * LOG2_E
        return log_1m, log_1m - neg_z

    def later_sums(log_1m):
        hi, lo = _split_bf16(log_1m)
        per_head = [_dot(jnp.concatenate([hi[:, h * blk:(h + 1) * blk], lo[:, h * blk:(h + 1) * blk]], axis=1),
                         later_and_ones) for h in range(2)]
        within = jnp.concatenate([per_head[0][:, :blk], per_head[1][:, :blk]], axis=1)
        total = jnp.concatenate([per_head[0][:, blk:], per_head[1][:, blk:]], axis=1)
        return within, total

    def later_sums_half(log_1m):
        hi, lo = _split_bf16(log_1m)
        r = _dot(jnp.concatenate([hi, lo], axis=1), later_and_ones_half)
        return r[:, :blk], r[:, blk:]

    def half_unit(q0, k0, run, keep):
        k = block_diag(k_ref[0, pl.ds(k0, half), :].astype(BF16), head0_lanes_half)
        v = block_diag(v_ref[0, pl.ds(k0, half), :].astype(BF16), head0_lanes_half)
        log_1m, log_b = logs(_dot_nt(load_neg_q(q0), k))
        if keep is not None:
            log_1m = jnp.where(keep, log_1m, 0.0)
        within, total = later_sums_half(log_1m)
        a = jnp.exp2(log_b + within + run)
        if keep is not None:
            a = jnp.where(keep, a, 0.0)
        return _dot(a.astype(BF16), v), run + total

    def qgroup(qg, carry0, edge):
        qbs = [qg * SB_QBLOCKS_PER_ITER + i for i in range(SB_QBLOCKS_PER_ITER)]
        q0s = [pl.multiple_of(qb * blk, blk) for qb in qbs]
        units = [(i, step) for i in range(SB_QBLOCKS_PER_ITER) for step in range(SB_FULL_BLOCKS)]
        halves = [(i, "half") for i in range(SB_QBLOCKS_PER_ITER)]
        keeps, vals, neg_z = {}, {}, {}
        for i, step in units:
            kb = qbs[i] - step
            k0 = pl.multiple_of(jnp.maximum(kb, 0) * blk, blk)
            keeps[i, step] = strictly_before if step == 0 else (kb >= 0 if edge else None)
            vals[i, step] = block_diag(v_ref[0, pl.ds(k0, blk), :].astype(BF16), head0_lanes)
            neg_z[i, step] = _dot_nt(load_neg_q(q0s[i]),
                                     block_diag(k_ref[0, pl.ds(k0, blk), :].astype(BF16), head0_lanes))
        for u in halves:
            first_key = q0s[u[0]] - SB_FULL_BLOCKS * blk + blk - half
            k0 = pl.multiple_of(jnp.maximum(first_key, 0), half)
            keeps[u] = first_key >= 0 if edge else None
            vals[u] = block_diag(v_ref[0, pl.ds(k0, half), :].astype(BF16), head0_lanes_half)
            neg_z[u] = _dot_nt(load_neg_q(q0s[u[0]]),
                               block_diag(k_ref[0, pl.ds(k0, half), :].astype(BF16), head0_lanes_half))
        log_bs, sums = {}, {}
        for u in units + halves:
            log_1m, log_bs[u] = logs(neg_z[u])
            sums[u] = log_1m if keeps[u] is None else jnp.where(keeps[u], log_1m, 0.0)
        for u in units:
            sums[u] = later_sums(sums[u])
        for u in halves:
            sums[u] = later_sums_half(sums[u])
        weights, tops = {}, {}
        for i in range(SB_QBLOCKS_PER_ITER):
            run = jnp.zeros((blk, 2 * blk), F32)
            for step in range(SB_FULL_BLOCKS):
                u = (i, step)
                within, total = sums[u]
                a = jnp.exp2(log_bs[u] + within + run)
                weights[u] = (a if keeps[u] is None else jnp.where(keeps[u], a, 0.0)).astype(BF16)
                run = run + total
            run = jnp.where(head0_keys_half, run[:, :blk], run[:, blk:])
            u = (i, "half")
            within, total = sums[u]
            a = jnp.exp2(log_bs[u] + within + run)
            weights[u] = (a if keeps[u] is None else jnp.where(keeps[u], a, 0.0)).astype(BF16)
            run = run + total
            run_ref[i] = run
            tops[i] = jnp.max(run)
        for i in range(SB_QBLOCKS_PER_ITER):
            acc = _dot(weights[i, 0], vals[i, 0])
            for step in list(range(1, SB_FULL_BLOCKS)) + ["half"]:
                acc = acc + _dot(weights[i, step], vals[i, step])
            acc_ref[i] = acc

        def cond(st):
            kh, top = st
            return jnp.logical_and(kh >= 0, top > SB_SKIP_LOG * LOG2_E)

        for i in range(SB_QBLOCKS_PER_ITER):
            def body(st, i=i):
                kh, _ = st
                part, run = half_unit(q0s[i], pl.multiple_of(kh * half, half), run_ref[i], None)
                acc_ref[i] = acc_ref[i] + part
                run_ref[i] = run
                return kh - 1, jnp.max(run)

            lax.while_loop(cond, body, (2 * (qbs[i] - SB_FULL_BLOCKS + 1) - 2, tops[i]))
            o_ref[0, pl.ds(q0s[i], blk), :] = acc_ref[i].astype(o_ref.dtype)
        return carry0

    first_full = -(-SB_FULL_BLOCKS // SB_QBLOCKS_PER_ITER)
    for qg in range(first_full):
        qgroup(qg, 0, True)
    lax.fori_loop(first_full, seq // (blk * SB_QBLOCKS_PER_ITER), functools.partial(qgroup, edge=False), 0)


def _stick_breaking(mix3):
    bsz, seq, _ = mix3.shape
    pairs = B_HEADS // 2
    w = 2 * B_HDIM
    return pl.pallas_call(
        _sb_kernel,
        grid=(bsz, pairs),
        in_specs=[pl.BlockSpec((1, seq, w), lambda b, p: (b, 0, OFF_BQ + p)),
                  pl.BlockSpec((1, seq, w), lambda b, p: (b, 0, OFF_BK + p)),
                  pl.BlockSpec((1, seq, w), lambda b, p: (b, 0, OFF_BV + p))],
        out_specs=pl.BlockSpec((1, seq, w), lambda b, p: (b, 0, p)),
        out_shape=jax.ShapeDtypeStruct((bsz, seq, B_W), BF16),
        scratch_shapes=[pltpu.VMEM((SB_QBLOCKS_PER_ITER, B_BLOCK, B_BLOCK), F32),
                        pltpu.VMEM((SB_QBLOCKS_PER_ITER, B_BLOCK, 2 * B_HDIM), F32)],
        compiler_params=_cparams(("parallel", "parallel")),
        name="stick_breaking",
    )(mix3, mix3, mix3)


def _dil_kernel(*refs, tile, slopes):
    blk, d = C_BLOCK, C_HDIM
    n_groups = len(C_GROUPS)
    o_ref, os_ref, ls_ref = refs[5 * n_groups:]
    pair = pl.program_id(1)
    first_tile = pl.program_id(2) == 0
    scale = d ** -0.5
    qi = lax.broadcasted_iota(jnp.int32, (blk, 4 * blk), 0)
    lane4 = lax.broadcasted_iota(jnp.int32, (blk, 4 * blk), 1)
    kj = lane4 % (2 * blk)
    head0_keys = lane4 < 2 * blk
    delta = qi + blk - kj
    head0_lanes = lax.broadcasted_iota(jnp.int32, (2 * blk, 2 * d), 1) < d
    head0_out = lax.broadcasted_iota(jnp.int32, (blk, 2 * d), 1) < d
    ones0 = head0_lanes.astype(BF16)
    ones1 = jnp.logical_not(head0_lanes).astype(BF16)

    def block_diag_keys(k2):
        zero = jnp.zeros_like(k2)
        return jnp.concatenate([jnp.where(head0_lanes, k2, zero), jnp.where(head0_lanes, zero, k2)], axis=0)

    def block_diag_values(v2):
        zero = jnp.zeros_like(v2)
        return jnp.concatenate([jnp.concatenate([jnp.where(head0_lanes, v2, zero), ones0], axis=1),
                                jnp.concatenate([jnp.where(head0_lanes, zero, v2), ones1], axis=1)], axis=0)

    for g, (window, dil) in enumerate(C_GROUPS):
        q_ref, k_ref, kp_ref, v_ref, vp_ref = refs[5 * g:5 * g + 5]
        window_ok = jnp.logical_and(delta >= 0, delta <= window // dil)
        window_ok_first = jnp.logical_and(window_ok, jnp.logical_or(kj >= blk, jnp.logical_not(first_tile)))
        slope0 = jnp.where(pair == 0, jnp.float32(slopes[g][0]), jnp.float32(slopes[g][2]))
        slope1 = jnp.where(pair == 0, jnp.float32(slopes[g][1]), jnp.float32(slopes[g][3]))
        alibi = jnp.where(head0_keys, slope0, slope1) * (delta * dil).astype(F32)

        def strided(ref, rho, block):
            start = rho + block * blk * dil
            return ref[0, pl.ds(start, blk, stride=dil), :] if dil > 1 else ref[0, pl.ds(start, blk), :]

        blocks = [(rho, i) for rho in range(dil) for i in range(tile // (dil * blk))]
        for at in range(0, len(blocks), DIL_BATCH):
            batch = blocks[at:at + DIL_BATCH]
            scores, values = {}, {}
            for rho, i in batch:
                k2 = jnp.concatenate([strided(kp_ref, rho, 0) if i == 0 else strided(k_ref, rho, i - 1),
                                      strided(k_ref, rho, i)], axis=0).astype(BF16)
                v2 = jnp.concatenate([strided(vp_ref, rho, 0) if i == 0 else strided(v_ref, rho, i - 1),
                                      strided(v_ref, rho, i)], axis=0).astype(BF16)
                scores[rho, i] = _dot_nt((strided(q_ref, rho, i) * scale).astype(BF16), block_diag_keys(k2))
                values[rho, i] = block_diag_values(v2)
            probs, tops = {}, {}
            for rho, i in batch:
                valid = window_ok_first if i == 0 else window_ok
                s = jnp.where(valid, scores[rho, i] - alibi, NEG_BIG)
                mx0 = jnp.max(s[:, :2 * blk], axis=-1, keepdims=True)
                mx1 = jnp.max(s[:, 2 * blk:], axis=-1, keepdims=True)
                probs[rho, i] = jnp.exp(s - jnp.where(head0_keys, mx0, mx1)).astype(BF16)
                tops[rho, i] = jnp.where(head0_out, mx0, mx1)
            for rho, i in batch:
                r = _dot(probs[rho, i], values[rho, i])
                den = r[:, 2 * d:]
                start = rho + i * blk * dil
                rows = pl.ds(start, blk, stride=dil) if dil > 1 else pl.ds(start, blk)
                os_ref[g, rows, :] = r[:, :2 * d] / den
                ls_ref[g, rows, :] = tops[rho, i] + jnp.log(den)
    lses = [ls_ref[g] for g in range(n_groups)]
    top = functools.reduce(jnp.maximum, lses)
    wts = [jnp.exp(l - top) for l in lses]
    num = functools.reduce(lambda a, b: a + b, [wts[g] * os_ref[g] for g in range(n_groups)])
    o_ref[0] = (num / functools.reduce(lambda a, b: a + b, wts)).astype(o_ref.dtype)


def _dilated(mix3):
    bsz, seq, _ = mix3.shape
    pairs = C_HEADS_PER_GROUP // 2
    w = 2 * C_HDIM
    tile = C_BLOCK * max(dil for _, dil in C_GROUPS)
    all_slopes = _alibi_slopes(C_HEADS)
    slopes = tuple(tuple(all_slopes[g * C_HEADS_PER_GROUP:(g + 1) * C_HEADS_PER_GROUP])
                   for g in range(len(C_GROUPS)))
    in_specs = []
    for g, (_, dil) in enumerate(C_GROUPS):
        back = C_BLOCK * dil
        per = tile // back

        def cur(off, g=g):
            return pl.BlockSpec((1, tile, w), lambda b, p, n: (b, n, off + pairs * g + p))

        def prev(off, g=g, back=back, per=per):
            return pl.BlockSpec((1, back, w), lambda b, p, n: (b, jnp.maximum(n * per - 1, 0), off + pairs * g + p))

        in_specs += [cur(OFF_CQ), cur(OFF_CK), prev(OFF_CK), cur(OFF_CV), prev(OFF_CV)]
    return pl.pallas_call(
        functools.partial(_dil_kernel, tile=tile, slopes=slopes),
        grid=(bsz, pairs, seq // tile),
        in_specs=in_specs,
        out_specs=pl.BlockSpec((1, tile, w), lambda b, p, n: (b, n, p)),
        out_shape=jax.ShapeDtypeStruct((bsz, seq, C_OUT), BF16),
        scratch_shapes=[pltpu.VMEM((len(C_GROUPS), tile, w), F32),
                        pltpu.VMEM((len(C_GROUPS), tile, w), F32)],
        compiler_params=_cparams(("parallel", "parallel", "arbitrary")),
        name="dilated",
    )(*([mix3] * (5 * len(C_GROUPS))))


def _mixout_kernel(x_ref, mod_ref, g_ref, ya_ref, yb_ref, yc_ref, gate_ref, wa_ref, wb_ref, wc_ref, wo_ref,
                   out_ref, *, sub):
    d = D_MODEL
    subs = [slice(s * sub, (s + 1) * sub) for s in range(x_ref.shape[0] // sub)]
    merged = []
    for rs in subs:
        m = gate_ref[rs, 0:d].astype(F32) * _dot(ya_ref[rs, :], wa_ref[...])
        m = m + gate_ref[rs, d:2 * d].astype(F32) * _dot(yb_ref[rs, :], wb_ref[...])
        m = m + gate_ref[rs, 2 * d:3 * d].astype(F32) * _dot(yc_ref[rs, :], wc_ref[...])
        merged.append(m.astype(BF16))
    ys = [_dot(m, wo_ref[...]) for m in merged]
    for rs, y in zip(subs, ys):
        out_ref[rs, :] = x_ref[rs, :] + mod_ref[0][2:3] * _rms(y, g_ref[...])


def _mixout(x, mod, g, ya, yb, yc, gates, wa, wb, wc, wo, seq, tm=1024, sub=512):
    t, d = x.shape
    per_b = seq // tm

    def tok(wd):
        return pl.BlockSpec((tm, wd), lambda i: (i, 0))

    return pl.pallas_call(
        functools.partial(_mixout_kernel, sub=sub),
        grid=(t // tm,),
        in_specs=[tok(d), pl.BlockSpec((1, 3, d), lambda i: (i // per_b, 0, 0)), _resident((1, d)),
                  tok(A_V), tok(B_W), tok(C_OUT), tok(GATE_COLS),
                  _resident(wa.shape), _resident(wb.shape), _resident(wc.shape), _resident(wo.shape)],
        out_specs=tok(d),
        out_shape=jax.ShapeDtypeStruct((t, d), F32),
        compiler_params=_cparams(("parallel",)),
        name="mix_out",
    )(x, mod, g, ya, yb, yc, gates, wa, wb, wc, wo)


def kernel(x, c, w_ada, b_ada, norm_g, ffn1_w_in, ffn1_w_out, w_in, hgrn_lb_logits, hgrn_norm_g,
           w_branch_a, w_branch_b, w_branch_c, w_out, ffn2_w_in, ffn2_w_out):
    bsz, seq, d = x.shape
    depth = w_ada.shape[0]
    lb_all = _lower_bounds(hgrn_lb_logits.astype(F32))
    mod = _ada(c, w_ada, b_ada).reshape(depth, bsz, 3, 3, d)
    xt = x.reshape(bsz * seq, d)
    for l in range(depth):
        xt = _ffn_sublayer(xt, mod[l, :, 0], norm_g[l, 0:2], ffn1_w_in[l].astype(BF16),
                           ffn1_w_out[l].astype(BF16), 0.5, seq)
        mix, gates = _inproj(xt, mod[l, :, 1], norm_g[l, 2:3], w_in[l].astype(BF16), seq)
        mix3 = mix.reshape(bsz, seq, MIX_COLS)
        ya = _hgrn(mix3, lb_all[l:l + 1], hgrn_norm_g[l:l + 1]).reshape(bsz * seq, A_V)
        yb = _stick_breaking(mix3).reshape(bsz * seq, B_W)
        yc = _dilated(mix3).reshape(bsz * seq, C_OUT)
        xt = _mixout(xt, mod[l, :, 1], norm_g[l, 3:4], ya, yb, yc, gates,
                     w_branch_a[l].astype(BF16), w_branch_b[l].astype(BF16), w_branch_c[l].astype(BF16),
                     w_out[l].astype(BF16), seq)
        xt = _ffn_sublayer(xt, mod[l, :, 2], norm_g[l, 4:6], ffn2_w_in[l].astype(BF16),
                           ffn2_w_out[l].astype(BF16), 0.5, seq)
    return xt.reshape(bsz, seq, d)
```

```python
import functools
import math

import jax
import jax.numpy as jnp
from jax import lax
from jax.experimental import pallas as pl
from jax.experimental.pallas import tpu as pltpu

D_MODEL = 1024
DEPTH = 4
A_HEADS = 6
A_KDIM = 128
A_VDIM = 64
A_CHUNK = 64
B_HEADS = 6
B_HDIM = 64
B_BLOCK = 128
C_GROUPS = ((128, 1), (512, 4), (2048, 16))
C_HEADS_PER_GROUP = 4
C_HEADS = C_HEADS_PER_GROUP * len(C_GROUPS)
C_HDIM = 64
C_BLOCK = 128
D_FF = 2816
N_BRANCH = 3
EPS = 1e-6
NEG_BIG = -1e30
TINY = 1e-30

A_QK = A_HEADS * A_KDIM
A_V = A_HEADS * A_VDIM
B_W = B_HEADS * B_HDIM
C_W = C_HEADS * C_HDIM
C_OUT = C_HEADS_PER_GROUP * C_HDIM
MIX_COLS = 2 * A_QK + 2 * A_V + 3 * B_W + 3 * C_W
GATE_COLS = N_BRANCH * D_MODEL
IN_COLS = MIX_COLS + GATE_COLS

LANES = 128
MXU_WIDTH = 256
OFF_AQ = 0
OFF_AF = A_QK // LANES
OFF_AI = 2 * A_QK // LANES
OFF_AG = (2 * A_QK + A_V) // LANES
OFF_BQ = (2 * A_QK + 2 * A_V) // LANES
OFF_BK = OFF_BQ + B_W // LANES
OFF_BV = OFF_BK + B_W // LANES
OFF_CQ = OFF_BV + B_W // LANES
OFF_CK = OFF_CQ + C_W // LANES
OFF_CV = OFF_CK + C_W // LANES
MIX_BLOCKS = MIX_COLS // LANES

VMEM_LIMIT = 56 * 1024 * 1024
SB_SKIP_LOG = -105.0
LOG2_E = 1.4426950408889634
SB_FULL_BLOCKS = 3
HGRN_BATCH = 32
SB_QBLOCKS_PER_ITER = 8
DIL_BATCH = 4
HGRN_SAFE_SPAN = 80.0

F32 = jnp.float32
BF16 = jnp.bfloat16


def _alibi_slopes(n):
    def pow2_slopes(m):
        start = 2.0 ** (-8.0 / m)
        return [start ** (i + 1) for i in range(m)]
    if math.log2(n).is_integer():
        s = pow2_slopes(n)
    else:
        c = 2 ** int(math.floor(math.log2(n)))
        s = pow2_slopes(c) + pow2_slopes(2 * c)[0::2][: n - c]
    return sorted(s, reverse=True)


def _rms(x, g):
    return x * lax.rsqrt(jnp.mean(x * x, axis=-1, keepdims=True) + EPS) * g


def _sigmoid(x):
    return 1.0 / (1.0 + jnp.exp(-x))


def _dot(a, b):
    return jnp.dot(a, b, preferred_element_type=F32)


def _dot_nt(a, b):
    return lax.dot_general(a, b, (((1,), (1,)), ((), ())), preferred_element_type=F32)


def _split_bf16(x):
    hi = x.astype(BF16)
    lo = (x - hi.astype(F32)).astype(BF16)
    return hi, lo


def _cparams(sem):
    return pltpu.CompilerParams(dimension_semantics=sem, vmem_limit_bytes=VMEM_LIMIT)


def _resident(shape):
    nd = len(shape)
    return pl.BlockSpec(shape, lambda *_: (0,) * nd, pipeline_mode=pl.Buffered(1))


def _ada_kernel(c_ref, w_ref, b_ref, o_ref):
    c = c_ref[...]
    ca = c * _sigmoid(c)
    o_ref[0] = jnp.dot(ca, w_ref[0], precision=lax.Precision.HIGHEST,
                       preferred_element_type=F32) + b_ref[0]


def _ada(c, w_ada, b_ada):
    depth, d, n = w_ada.shape
    bsz = c.shape[0]
    tn = 1152
    return pl.pallas_call(
        _ada_kernel,
        grid=(depth, n // tn),
        in_specs=[pl.BlockSpec((bsz, d), lambda l, j: (0, 0)),
                  pl.BlockSpec((1, d, tn), lambda l, j: (l, 0, j)),
                  pl.BlockSpec((1, 1, tn), lambda l, j: (l, 0, j))],
        out_specs=pl.BlockSpec((1, bsz, tn), lambda l, j: (l, 0, j)),
        out_shape=jax.ShapeDtypeStruct((depth, bsz, n), F32),
        compiler_params=_cparams(("parallel", "parallel")),
        name="ada_mod",
    )(c, w_ada, b_ada.reshape(depth, 1, n))


def _lb_kernel(l_ref, o_ref):
    x = l_ref[...]
    e = jnp.exp(x - jnp.max(x, axis=0, keepdims=True))
    p = e / jnp.sum(e, axis=0, keepdims=True)
    depth = x.shape[0]
    run = p[0:1]
    o_ref[0:1, :] = run - p[0:1]
    for l in range(1, depth):
        run = run + p[l:l + 1]
        o_ref[l:l + 1, :] = run - p[0:1]


def _lower_bounds(logits):
    return pl.pallas_call(_lb_kernel, out_shape=jax.ShapeDtypeStruct(logits.shape, F32),
                          name="hgrn_lower_bounds")(logits)


def _ffn_kernel(x_ref, mod_ref, g_ref, win_ref, wout_ref, o_ref, act_ref, *, res_w, ffc, sub):
    m = mod_ref[0]
    d_ff = wout_ref.shape[0]
    subs = [slice(s * sub, (s + 1) * sub) for s in range(x_ref.shape[0] // sub)]
    hb = [(_rms(x_ref[rs, :], g_ref[0:1]) * (1.0 + m[1:2]) + m[0:1]).astype(BF16) for rs in subs]
    for s, rs in enumerate(subs):
        for c in range(d_ff // ffc):
            a = _dot(hb[s], win_ref[:, c * ffc:(c + 1) * ffc])
            b = _dot(hb[s], win_ref[:, d_ff + c * ffc:d_ff + (c + 1) * ffc])
            act_ref[rs, c * ffc:(c + 1) * ffc] = (a * _sigmoid(a) * b).astype(BF16)
    ys = [_dot(act_ref[rs, :], wout_ref[...]) for rs in subs]
    for s, rs in enumerate(subs):
        o_ref[rs, :] = x_ref[rs, :] + res_w * m[2:3] * _rms(ys[s], g_ref[1:2])


def _ffn_sublayer(x, mod, g2, w_in, w_out, res_w, seq, tm=1024, ffc=MXU_WIDTH, sub=512):
    t, d = x.shape
    d_ff = w_out.shape[0]
    per_b = seq // tm
    return pl.pallas_call(
        functools.partial(_ffn_kernel, res_w=res_w, ffc=ffc, sub=sub),
        grid=(t // tm,),
        in_specs=[pl.BlockSpec((tm, d), lambda i: (i, 0)),
                  pl.BlockSpec((1, 3, d), lambda i: (i // per_b, 0, 0)),
                  _resident((2, d)),
                  _resident((d, 2 * d_ff)),
                  _resident((d_ff, d))],
        out_specs=pl.BlockSpec((tm, d), lambda i: (i, 0)),
        out_shape=jax.ShapeDtypeStruct((t, d), F32),
        scratch_shapes=[pltpu.VMEM((tm, d_ff), BF16)],
        compiler_params=_cparams(("parallel",)),
        name="ffn_sublayer",
    )(x, mod, g2, w_in, w_out)


def _inproj_kernel(x_ref, mod_ref, g_ref, w_ref, mix_ref, gate_ref, *, cc, sub):
    m = mod_ref[0]
    subs = [slice(s * sub, (s + 1) * sub) for s in range(x_ref.shape[0] // sub)]
    hb = [(_rms(x_ref[rs, :], g_ref[...]) * (1.0 + m[1:2]) + m[0:1]).astype(BF16) for rs in subs]
    for s, rs in enumerate(subs):
        for lo in range(0, IN_COLS, cc):
            hi = min(lo + cc, IN_COLS)
            z = _dot(hb[s], w_ref[:, lo:hi])
            mix_hi = min(hi, MIX_COLS)
            if lo < mix_hi:
                mix_ref[rs, lo:mix_hi] = z[:, :mix_hi - lo]
            if hi > MIX_COLS:
                g_lo = max(lo, MIX_COLS)
                gate_ref[rs, g_lo - MIX_COLS:hi - MIX_COLS] = _sigmoid(z[:, g_lo - lo:]).astype(BF16)


def _inproj(x, mod, g, w_in, seq, tm=512, cc=MXU_WIDTH, sub=256):
    t, d = x.shape
    per_b = seq // tm
    return pl.pallas_call(
        functools.partial(_inproj_kernel, cc=cc, sub=sub),
        grid=(t // tm,),
        in_specs=[pl.BlockSpec((tm, d), lambda i: (i, 0)),
                  pl.BlockSpec((1, 3, d), lambda i: (i // per_b, 0, 0)),
                  _resident((1, d)),
                  _resident((d, IN_COLS))],
        out_specs=[pl.BlockSpec((tm, MIX_COLS), lambda i: (i, 0)),
                   pl.BlockSpec((tm, GATE_COLS), lambda i: (i, 0))],
        out_shape=[jax.ShapeDtypeStruct((t, MIX_COLS), F32),
                   jax.ShapeDtypeStruct((t, GATE_COLS), BF16)],
        compiler_params=_cparams(("parallel",)),
        name="mix_inproj",
    )(x, mod, g, w_in)


def _hgrn_kernel(q_ref, f_ref, i_ref, g_ref, lb_ref, ng_ref, o_ref,
                 state_ref, qs_ref, ks_ref, bs_ref, oi_ref, *, ts):
    ck, kd, vd = A_CHUNK, A_KDIM, A_VDIM
    nc = ts // ck
    mid = ck // 2 - 1

    @pl.when(pl.program_id(2) == 0)
    def _():
        state_ref[...] = jnp.zeros_like(state_ref)

    def iota(shape, axis):
        return lax.broadcasted_iota(jnp.int32, shape, axis)

    tri2 = (iota((ck, 2 * ck), 0) >= iota((ck, 2 * ck), 1) % ck).astype(BF16)
    row1 = iota((ck, 1), 0)
    ng = ng_ref[...]
    ng2 = jnp.concatenate([ng, ng], axis=1)
    rows_of = lambda c: slice(c * ck, (c + 1) * ck)

    qraw = q_ref[0]
    fr = f_ref[0]
    lb = lb_ref[...]
    qs = qraw * _sigmoid(qraw)
    e = jnp.exp(-jnp.abs(fr))
    big = 1.0 / (1.0 + e)
    small = e * big
    pos = fr >= 0.0
    f = lb + (1.0 - lb) * jnp.where(pos, big, small)
    k = (1.0 - lb) * jnp.where(pos, small, big)
    lf_hi, lf_lo = _split_bf16(jnp.log(jnp.maximum(f, TINY)))
    b = jnp.concatenate(
        [_dot(tri2, jnp.concatenate([lf_hi[rows_of(c)], lf_lo[rows_of(c)]], axis=0)) for c in range(nc)],
        axis=0)
    worst = None
    for c in range(nc):
        b_mid = b[c * ck + mid:c * ck + mid + 1]
        b_end = b[(c + 1) * ck - 1:(c + 1) * ck]
        w = jnp.minimum(b_mid, b_end - b_mid)
        worst = w if worst is None else jnp.minimum(worst, w)
    safe = jnp.min(worst) > -HGRN_SAFE_SPAN

    @pl.when(safe)
    def _():
        key_head0 = iota((ck, 2 * kd), 1) < kd
        val_head0 = iota((ck, 2 * vd), 1) < vd
        causal2 = iota((ck, 2 * ck), 0) >= iota((ck, 2 * ck), 1) % ck
        r4 = iota((4 * vd, 2 * vd), 0) % (2 * vd)
        same_head_sum = ((r4 < vd) == (iota((4 * vd, 2 * vd), 1) < vd)).astype(BF16)

        def block_diag(x, head0):
            zero = jnp.zeros_like(x)
            return jnp.concatenate([jnp.where(head0, x, zero), jnp.where(head0, zero, x)], axis=0)

        sts = [state_ref[0:vd, 0:kd], state_ref[vd:, kd:]]
        zero_block = jnp.zeros((vd, kd), BF16)
        for first in range(0, nc, HGRN_BATCH):
            cs = range(first, min(first + HGRN_BATCH, nc))
            qa, ka, q_dec, k_end, end_decay, vb = {}, {}, {}, {}, {}, {}
            for c in cs:
                bc = b[rows_of(c)]
                b_mid = bc[mid:mid + 1]
                b_end = bc[ck - 1:ck]
                qa_f = qs[rows_of(c)] * jnp.exp(bc - b_mid)
                ka_f = k[rows_of(c)] * jnp.exp(b_mid - bc)
                qa[c] = qa_f.astype(BF16)
                ka[c] = block_diag(ka_f.astype(BF16), key_head0)
                q_dec[c] = (qa_f * jnp.exp(b_mid)).astype(BF16)
                k_end[c] = (ka_f * jnp.exp(b_end - b_mid)).astype(BF16)
                end_decay[c] = jnp.exp(b_end)
                vb[c] = i_ref[0, rows_of(c), :].astype(BF16)
            scores = {c: jnp.where(causal2, _dot_nt(qa[c], ka[c]), 0.0).astype(BF16) for c in cs}
            o_intra = {c: _dot(scores[c], block_diag(vb[c], val_head0)) for c in cs}
            update = {c: lax.dot_general(vb[c], k_end[c], (((0,), (0,)), ((), ())), preferred_element_type=F32)
                      for c in cs}
            state_before = {}
            for c in cs:
                state_before[c] = jnp.concatenate(
                    [jnp.concatenate([sts[0].astype(BF16), zero_block], axis=1),
                     jnp.concatenate([zero_block, sts[1].astype(BF16)], axis=1)], axis=0)
                sts = [end_decay[c][:, 0:kd] * sts[0] + update[c][0:vd, 0:kd],
                       end_decay[c][:, kd:] * sts[1] + update[c][vd:, kd:]]
            outs = {c: o_intra[c] + _dot_nt(q_dec[c], state_before[c]) for c in cs}
            squares = {}
            for c in cs:
                hi, lo = _split_bf16(outs[c] * outs[c])
                squares[c] = _dot(jnp.concatenate([hi, lo], axis=1), same_head_sum)
            for c in cs:
                gate = g_ref[0, rows_of(c), :]
                y = outs[c] * lax.rsqrt(squares[c] * (1.0 / vd) + EPS) * ng2
                o_ref[0, rows_of(c), :] = (y * (gate * _sigmoid(gate))).astype(o_ref.dtype)
        state_ref[0:vd, 0:kd] = sts[0]
        state_ref[vd:, kd:] = sts[1]

    @pl.when(jnp.logical_not(safe))
    def _():
        for j in range(2):
            ksl = slice(j * kd, (j + 1) * kd)
            vsl = slice(j * vd, (j + 1) * vd)
            qs_ref[j] = qs[:, ksl]
            ks_ref[j] = k[:, ksl]
            bs_ref[j] = b[:, ksl]

            def chunk(c, carry):
                r0 = pl.multiple_of(c * ck, ck)
                rows = pl.ds(r0, ck)
                bc = bs_ref[j, rows, :]
                qc = qs_ref[j, rows, :]
                kc = ks_ref[j, rows, :]
                v = i_ref[0, rows, vsl]
                vb = v.astype(BF16)

                def one_row(t, cr):
                    bt = bs_ref[j, pl.ds(r0 + t, 1), :]
                    qt = qs_ref[j, pl.ds(r0 + t, 1), :]
                    dec = jnp.exp(jnp.where(row1 <= t, bt - bc, NEG_BIG))
                    sc = jnp.sum(dec * qt * kc, axis=-1, keepdims=True)
                    oi_ref[pl.ds(t, 1), :] = jnp.sum(sc * v, axis=0, keepdims=True)
                    return cr

                lax.fori_loop(0, ck, one_row, 0)
                b_end = bc[ck - 1:ck]
                st = state_ref[vsl, ksl]
                o = oi_ref[...] + _dot_nt((qc * jnp.exp(bc)).astype(BF16), st.astype(BF16))
                k_end = (kc * jnp.exp(b_end - bc)).astype(BF16)
                state_ref[vsl, ksl] = jnp.exp(b_end) * st + lax.dot_general(
                    vb, k_end, (((0,), (0,)), ((), ())), preferred_element_type=F32)
                gate = g_ref[0, rows, vsl]
                o_ref[0, rows, vsl] = (_rms(o, ng) * (gate * _sigmoid(gate))).astype(o_ref.dtype)
                return carry

            lax.fori_loop(0, nc, chunk, 0)


def _hgrn(mix3, lb, ng, ts=2048):
    bsz, seq, _ = mix3.shape
    pairs = A_HEADS // 2
    kw, vw = 2 * A_KDIM, 2 * A_VDIM
    return pl.pallas_call(
        functools.partial(_hgrn_kernel, ts=ts),
        grid=(bsz, pairs, seq // ts),
        in_specs=[pl.BlockSpec((1, ts, kw), lambda b, p, s: (b, s, OFF_AQ * LANES // kw + p)),
                  pl.BlockSpec((1, ts, kw), lambda b, p, s: (b, s, OFF_AF * LANES // kw + p)),
                  pl.BlockSpec((1, ts, vw), lambda b, p, s: (b, s, OFF_AI + p)),
                  pl.BlockSpec((1, ts, vw), lambda b, p, s: (b, s, OFF_AG + p)),
                  pl.BlockSpec((1, kw), lambda b, p, s: (0, p)),
                  pl.BlockSpec((1, A_VDIM), lambda b, p, s: (0, 0))],
        out_specs=pl.BlockSpec((1, ts, vw), lambda b, p, s: (b, s, p)),
        out_shape=jax.ShapeDtypeStruct((bsz, seq, A_V), BF16),
        scratch_shapes=[pltpu.VMEM((2 * A_VDIM, 2 * A_KDIM), F32),
                        pltpu.VMEM((2, ts, A_KDIM), F32),
                        pltpu.VMEM((2, ts, A_KDIM), F32),
                        pltpu.VMEM((2, ts, A_KDIM), F32),
                        pltpu.VMEM((A_CHUNK, A_VDIM), F32)],
        compiler_params=_cparams(("parallel", "parallel", "arbitrary")),
        name="hgrn2",
    )(mix3, mix3, mix3, mix3, lb, ng)


def _sb_kernel(q_ref, k_ref, v_ref, o_ref, run_ref, acc_ref):
    blk, d = B_BLOCK, B_HDIM
    seq = q_ref.shape[1]
    neg_scale = -(d ** -0.5) * LOG2_E

    def iota(shape, axis):
        return lax.broadcasted_iota(jnp.int32, shape, axis)

    strictly_before = iota((blk, 2 * blk), 1) % blk < iota((blk, 2 * blk), 0)
    r2, c2 = iota((2 * blk, 2 * blk), 0), iota((2 * blk, 2 * blk), 1)
    later_and_ones = jnp.logical_or(r2 % blk > c2, c2 >= blk).astype(BF16)
    head0_lanes = iota((blk, 2 * d), 1) < d

    def load_neg_q(q0):
        return (q_ref[0, pl.ds(q0, blk), :] * neg_scale).astype(BF16)

    def block_diag(x):
        zero = jnp.zeros_like(x)
        return jnp.concatenate([jnp.where(head0_lanes, x, zero), jnp.where(head0_lanes, zero, x)], axis=0)

    def logs(neg_z):
        log_1m = jnp.minimum(neg_z, 0.0) - jnp.log(1.0 + jnp.exp2(-jnp.abs(neg_z))) * LOG2_E
        return log_1m, log_1m - neg_z

    def later_sums(log_1m):
        hi, lo = _split_bf16(log_1m)
        per_head = [_dot(jnp.concatenate([hi[:, h * blk:(h + 1) * blk], lo[:, h * blk:(h + 1) * blk]], axis=1),
                         later_and_ones) for h in range(2)]
        within = jnp.concatenate([per_head[0][:, :blk], per_head[1][:, :blk]], axis=1)
        total = jnp.concatenate([per_head[0][:, blk:], per_head[1][:, blk:]], axis=1)
        return within, total

    def qgroup(qg, carry0, edge):
        qbs = [qg * SB_QBLOCKS_PER_ITER + i for i in range(SB_QBLOCKS_PER_ITER)]
        q0s = [pl.multiple_of(qb * blk, blk) for qb in qbs]
        units = [(i, step) for i in range(SB_QBLOCKS_PER_ITER) for step in range(SB_FULL_BLOCKS)]
        keeps, vals, neg_z = {}, {}, {}
        for i, step in units:
            kb = qbs[i] - step
            k0 = pl.multiple_of(jnp.maximum(kb, 0) * blk, blk)
            keeps[i, step] = strictly_before if step == 0 else (kb >= 0 if edge else None)
            vals[i, step] = block_diag(v_ref[0, pl.ds(k0, blk), :].astype(BF16))
            neg_z[i, step] = _dot_nt(load_neg_q(q0s[i]), block_diag(k_ref[0, pl.ds(k0, blk), :].astype(BF16)))
        log_bs, sums = {}, {}
        for u in units:
            log_1m, log_bs[u] = logs(neg_z[u])
            sums[u] = log_1m if keeps[u] is None else jnp.where(keeps[u], log_1m, 0.0)
        for u in units:
            sums[u] = later_sums(sums[u])
        weights, tops = {}, {}
        for i in range(SB_QBLOCKS_PER_ITER):
            run = jnp.zeros((blk, 2 * blk), F32)
            for step in range(SB_FULL_BLOCKS):
                u = (i, step)
                within, total = sums[u]
                a = jnp.exp2(log_bs[u] + within + run)
                weights[u] = (a if keeps[u] is None else jnp.where(keeps[u], a, 0.0)).astype(BF16)
                run = run + total
            run_ref[i] = run
            tops[i] = jnp.max(run)
        for i in range(SB_QBLOCKS_PER_ITER):
            acc = _dot(weights[i, 0], vals[i, 0])
            for step in range(1, SB_FULL_BLOCKS):
                acc = acc + _dot(weights[i, step], vals[i, step])
            acc_ref[i] = acc

        def cond(st):
            kb, top = st
            return jnp.logical_and(kb >= 0, top > SB_SKIP_LOG * LOG2_E)

        for i in range(SB_QBLOCKS_PER_ITER):
            def body(st, i=i):
                kb, _ = st
                k0 = pl.multiple_of(kb * blk, blk)
                log_1m, log_b = logs(_dot_nt(load_neg_q(q0s[i]),
                                             block_diag(k_ref[0, pl.ds(k0, blk), :].astype(BF16))))
                within, total = later_sums(log_1m)
                run = run_ref[i]
                a = jnp.exp2(log_b + within + run)
                acc_ref[i] = acc_ref[i] + _dot(a.astype(BF16), block_diag(v_ref[0, pl.ds(k0, blk), :].astype(BF16)))
                run = run + total
                run_ref[i] = run
                return kb - 1, jnp.max(run)

            lax.while_loop(cond, body, (qbs[i] - SB_FULL_BLOCKS, tops[i]))
            o_ref[0, pl.ds(q0s[i], blk), :] = acc_ref[i].astype(o_ref.dtype)
        return carry0

    first_full = -(-(SB_FULL_BLOCKS - 1) // SB_QBLOCKS_PER_ITER)
    for qg in range(first_full):
        qgroup(qg, 0, True)
    lax.fori_loop(first_full, seq // (blk * SB_QBLOCKS_PER_ITER), functools.partial(qgroup, edge=False), 0)


def _stick_breaking(mix3):
    bsz, seq, _ = mix3.shape
    pairs = B_HEADS // 2
    w = 2 * B_HDIM
    return pl.pallas_call(
        _sb_kernel,
        grid=(bsz, pairs),
        in_specs=[pl.BlockSpec((1, seq, w), lambda b, p: (b, 0, OFF_BQ + p)),
                  pl.BlockSpec((1, seq, w), lambda b, p: (b, 0, OFF_BK + p)),
                  pl.BlockSpec((1, seq, w), lambda b, p: (b, 0, OFF_BV + p))],
        out_specs=pl.BlockSpec((1, seq, w), lambda b, p: (b, 0, p)),
        out_shape=jax.ShapeDtypeStruct((bsz, seq, B_W), BF16),
        scratch_shapes=[pltpu.VMEM((SB_QBLOCKS_PER_ITER, B_BLOCK, 2 * B_BLOCK), F32),
                        pltpu.VMEM((SB_QBLOCKS_PER_ITER, B_BLOCK, 2 * B_HDIM), F32)],
        compiler_params=_cparams(("parallel", "parallel")),
        name="stick_breaking",
    )(mix3, mix3, mix3)


def _dil_kernel(*refs, tile, slopes):
    blk, d = C_BLOCK, C_HDIM
    n_groups = len(C_GROUPS)
    o_ref, os_ref, ls_ref = refs[5 * n_groups:]
    pair = pl.program_id(1)
    first_tile = pl.program_id(2) == 0
    scale = d ** -0.5
    qi = lax.broadcasted_iota(jnp.int32, (blk, 4 * blk), 0)
    lane4 = lax.broadcasted_iota(jnp.int32, (blk, 4 * blk), 1)
    kj = lane4 % (2 * blk)
    head0_keys = lane4 < 2 * blk
    delta = qi + blk - kj
    head0_lanes = lax.broadcasted_iota(jnp.int32, (2 * blk, 2 * d), 1) < d
    head0_out = lax.broadcasted_iota(jnp.int32, (blk, 2 * d), 1) < d
    ones0 = head0_lanes.astype(BF16)
    ones1 = jnp.logical_not(head0_lanes).astype(BF16)

    def block_diag_keys(k2):
        zero = jnp.zeros_like(k2)
        return jnp.concatenate([jnp.where(head0_lanes, k2, zero), jnp.where(head0_lanes, zero, k2)], axis=0)

    def block_diag_values(v2):
        zero = jnp.zeros_like(v2)
        return jnp.concatenate([jnp.concatenate([jnp.where(head0_lanes, v2, zero), ones0], axis=1),
                                jnp.concatenate([jnp.where(head0_lanes, zero, v2), ones1], axis=1)], axis=0)

    for g, (window, dil) in enumerate(C_GROUPS):
        q_ref, k_ref, kp_ref, v_ref, vp_ref = refs[5 * g:5 * g + 5]
        window_ok = jnp.logical_and(delta >= 0, delta <= window // dil)
        window_ok_first = jnp.logical_and(window_ok, jnp.logical_or(kj >= blk, jnp.logical_not(first_tile)))
        slope0 = jnp.where(pair == 0, jnp.float32(slopes[g][0]), jnp.float32(slopes[g][2]))
        slope1 = jnp.where(pair == 0, jnp.float32(slopes[g][1]), jnp.float32(slopes[g][3]))
        alibi = jnp.where(head0_keys, slope0, slope1) * (delta * dil).astype(F32)

        def strided(ref, rho, block):
            start = rho + block * blk * dil
            return ref[0, pl.ds(start, blk, stride=dil), :] if dil > 1 else ref[0, pl.ds(start, blk), :]

        blocks = [(rho, i) for rho in range(dil) for i in range(tile // (dil * blk))]
        for at in range(0, len(blocks), DIL_BATCH):
            batch = blocks[at:at + DIL_BATCH]
            scores, values = {}, {}
            for rho, i in batch:
                k2 = jnp.concatenate([strided(kp_ref, rho, 0) if i == 0 else strided(k_ref, rho, i - 1),
                                      strided(k_ref, rho, i)], axis=0).astype(BF16)
                v2 = jnp.concatenate([strided(vp_ref, rho, 0) if i == 0 else strided(v_ref, rho, i - 1),
                                      strided(v_ref, rho, i)], axis=0).astype(BF16)
                scores[rho, i] = _dot_nt((strided(q_ref, rho, i) * scale).astype(BF16), block_diag_keys(k2))
                values[rho, i] = block_diag_values(v2)
            probs, tops = {}, {}
            for rho, i in batch:
                valid = window_ok_first if i == 0 else window_ok
                s = jnp.where(valid, scores[rho, i] - alibi, NEG_BIG)
                mx0 = jnp.max(s[:, :2 * blk], axis=-1, keepdims=True)
                mx1 = jnp.max(s[:, 2 * blk:], axis=-1, keepdims=True)
                probs[rho, i] = jnp.exp(s - jnp.where(head0_keys, mx0, mx1)).astype(BF16)
                tops[rho, i] = jnp.where(head0_out, mx0, mx1)
            for rho, i in batch:
                r = _dot(probs[rho, i], values[rho, i])
                den = r[:, 2 * d:]
                start = rho + i * blk * dil
                rows = pl.ds(start, blk, stride=dil) if dil > 1 else pl.ds(start, blk)
                os_ref[g, rows, :] = r[:, :2 * d] / den
                ls_ref[g, rows, :] = tops[rho, i] + jnp.log(den)
    lses = [ls_ref[g] for g in range(n_groups)]
    top = functools.reduce(jnp.maximum, lses)
    wts = [jnp.exp(l - top) for l in lses]
    num = functools.reduce(lambda a, b: a + b, [wts[g] * os_ref[g] for g in range(n_groups)])
    o_ref[0] = (num / functools.reduce(lambda a, b: a + b, wts)).astype(o_ref.dtype)


def _dilated(mix3):
    bsz, seq, _ = mix3.shape
    pairs = C_HEADS_PER_GROUP // 2
    w = 2 * C_HDIM
    tile = C_BLOCK * max(dil for _, dil in C_GROUPS)
    all_slopes = _alibi_slopes(C_HEADS)
    slopes = tuple(tuple(all_slopes[g * C_HEADS_PER_GROUP:(g + 1) * C_HEADS_PER_GROUP])
                   for g in range(len(C_GROUPS)))
    in_specs = []
    for g, (_, dil) in enumerate(C_GROUPS):
        back = C_BLOCK * dil
        per = tile // back

        def cur(off, g=g):
            return pl.BlockSpec((1, tile, w), lambda b, p, n: (b, n, off + pairs * g + p))

        def prev(off, g=g, back=back, per=per):
            return pl.BlockSpec((1, back, w), lambda b, p, n: (b, jnp.maximum(n * per - 1, 0), off + pairs * g + p))

        in_specs += [cur(OFF_CQ), cur(OFF_CK), prev(OFF_CK), cur(OFF_CV), prev(OFF_CV)]
    return pl.pallas_call(
        functools.partial(_dil_kernel, tile=tile, slopes=slopes),
        grid=(bsz, pairs, seq // tile),
        in_specs=in_specs,
        out_specs=pl.BlockSpec((1, tile, w), lambda b, p, n: (b, n, p)),
        out_shape=jax.ShapeDtypeStruct((bsz, seq, C_OUT), BF16),
        scratch_shapes=[pltpu.VMEM((len(C_GROUPS), tile, w), F32),
                        pltpu.VMEM((len(C_GROUPS), tile, w), F32)],
        compiler_params=_cparams(("parallel", "parallel", "arbitrary")),
        name="dilated",
    )(*([mix3] * (5 * len(C_GROUPS))))


def _mixout_kernel(x_ref, mod_ref, g_ref, ya_ref, yb_ref, yc_ref, gate_ref, wa_ref, wb_ref, wc_ref, wo_ref,
                   out_ref, *, sub):
    d = D_MODEL
    subs = [slice(s * sub, (s + 1) * sub) for s in range(x_ref.shape[0] // sub)]
    merged = []
    for rs in subs:
        m = gate_ref[rs, 0:d].astype(F32) * _dot(ya_ref[rs, :], wa_ref[...])
        m = m + gate_ref[rs, d:2 * d].astype(F32) * _dot(yb_ref[rs, :], wb_ref[...])
        m = m + gate_ref[rs, 2 * d:3 * d].astype(F32) * _dot(yc_ref[rs, :], wc_ref[...])
        merged.append(m.astype(BF16))
    ys = [_dot(m, wo_ref[...]) for m in merged]
    for rs, y in zip(subs, ys):
        out_ref[rs, :] = x_ref[rs, :] + mod_ref[0][2:3] * _rms(y, g_ref[...])


def _mixout(x, mod, g, ya, yb, yc, gates, wa, wb, wc, wo, seq, tm=1024, sub=512):
    t, d = x.shape
    per_b = seq // tm

    def tok(wd):
        return pl.BlockSpec((tm, wd), lambda i: (i, 0))

    return pl.pallas_call(
        functools.partial(_mixout_kernel, sub=sub),
        grid=(t // tm,),
        in_specs=[tok(d), pl.BlockSpec((1, 3, d), lambda i: (i // per_b, 0, 0)), _resident((1, d)),
                  tok(A_V), tok(B_W), tok(C_OUT), tok(GATE_COLS),
                  _resident(wa.shape), _resident(wb.shape), _resident(wc.shape), _resident(wo.shape)],
        out_specs=tok(d),
        out_shape=jax.ShapeDtypeStruct((t, d), F32),
        compiler_params=_cparams(("parallel",)),
        name="mix_out",
    )(x, mod, g, ya, yb, yc, gates, wa, wb, wc, wo)


def kernel(x, c, w_ada, b_ada, norm_g, ffn1_w_in, ffn1_w_out, w_in, hgrn_lb_logits, hgrn_norm_g,
           w_branch_a, w_branch_b, w_branch_c, w_out, ffn2_w_in, ffn2_w_out):
    bsz, seq, d = x.shape
    depth = w_ada.shape[0]
    lb_all = _lower_bounds(hgrn_lb_logits.astype(F32))
    mod = _ada(c, w_ada, b_ada).reshape(depth, bsz, 3, 3, d)
    xt = x.reshape(bsz * seq, d)
    for l in range(depth):
        xt = _ffn_sublayer(xt, mod[l, :, 0], norm_g[l, 0:2], ffn1_w_in[l].astype(BF16),
                           ffn1_w_out[l].astype(BF16), 0.5, seq)
        mix, gates = _inproj(xt, mod[l, :, 1], norm_g[l, 2:3], w_in[l].astype(BF16), seq)
        mix3 = mix.reshape(bsz, seq, MIX_COLS)
        ya = _hgrn(mix3, lb_all[l:l + 1], hgrn_norm_g[l:l + 1]).reshape(bsz * seq, A_V)
        yb = _stick_breaking(mix3).reshape(bsz * seq, B_W)
        yc = _dilated(mix3).reshape(bsz * seq, C_OUT)
        xt = _mixout(xt, mod[l, :, 1], norm_g[l, 3:4], ya, yb, yc, gates,
                     w_branch_a[l].astype(BF16), w_branch_b[l].astype(BF16), w_branch_c[l].astype(BF16),
                     w_out[l].astype(BF16), seq)
        xt = _ffn_sublayer(xt, mod[l, :, 2], norm_g[l, 4:6], ffn2_w_in[l].astype(BF16),
                           ffn2_w_out[l].astype(BF16), 0.5, seq)
    return xt.reshape(bsz, seq, d)
```

```python
import functools
import math

import jax
import jax.numpy as jnp
from jax import lax
from jax.experimental import pallas as pl
from jax.experimental.pallas import tpu as pltpu

D_MODEL = 1024
DEPTH = 4
A_HEADS = 6
A_KDIM = 128
A_VDIM = 64
A_CHUNK = 64
B_HEADS = 6
B_HDIM = 64
B_BLOCK = 128
C_GROUPS = ((128, 1), (512, 4), (2048, 16))
C_HEADS_PER_GROUP = 4
C_HEADS = C_HEADS_PER_GROUP * len(C_GROUPS)
C_HDIM = 64
C_BLOCK = 128
D_FF = 2816
N_BRANCH = 3
EPS = 1e-6
NEG_BIG = -1e30
TINY = 1e-30

A_QK = A_HEADS * A_KDIM
A_V = A_HEADS * A_VDIM
B_W = B_HEADS * B_HDIM
C_W = C_HEADS * C_HDIM
C_OUT = C_HEADS_PER_GROUP * C_HDIM
MIX_COLS = 2 * A_QK + 2 * A_V + 3 * B_W + 3 * C_W
GATE_COLS = N_BRANCH * D_MODEL
IN_COLS = MIX_COLS + GATE_COLS

LANES = 128
MXU_WIDTH = 256
OFF_AQ = 0
OFF_AF = A_QK // LANES
OFF_AI = 2 * A_QK // LANES
OFF_AG = (2 * A_QK + A_V) // LANES
OFF_BQ = (2 * A_QK + 2 * A_V) // LANES
OFF_BK = OFF_BQ + B_W // LANES
OFF_BV = OFF_BK + B_W // LANES
OFF_CQ = OFF_BV + B_W // LANES
OFF_CK = OFF_CQ + C_W // LANES
OFF_CV = OFF_CK + C_W // LANES
MIX_BLOCKS = MIX_COLS // LANES

VMEM_LIMIT = 56 * 1024 * 1024
SB_SKIP_LOG = -105.0
LOG2_E = 1.4426950408889634
SB_FULL_BLOCKS = 3
HGRN_BATCH = 32
SB_QBLOCKS_PER_ITER = 8
DIL_MAX_STRIDE = 4
DIL_BATCH = 4
HGRN_SAFE_SPAN = 80.0

F32 = jnp.float32
BF16 = jnp.bfloat16


def _alibi_slopes(n):
    def pow2_slopes(m):
        start = 2.0 ** (-8.0 / m)
        return [start ** (i + 1) for i in range(m)]
    if math.log2(n).is_integer():
        s = pow2_slopes(n)
    else:
        c = 2 ** int(math.floor(math.log2(n)))
        s = pow2_slopes(c) + pow2_slopes(2 * c)[0::2][: n - c]
    return sorted(s, reverse=True)


def _rms(x, g):
    return x * lax.rsqrt(jnp.mean(x * x, axis=-1, keepdims=True) + EPS) * g


def _sigmoid(x):
    return 1.0 / (1.0 + jnp.exp(-x))


def _dot(a, b):
    return jnp.dot(a, b, preferred_element_type=F32)


def _dot_nt(a, b):
    return lax.dot_general(a, b, (((1,), (1,)), ((), ())), preferred_element_type=F32)


def _split_bf16(x):
    hi = x.astype(BF16)
    lo = (x - hi.astype(F32)).astype(BF16)
    return hi, lo


def _cparams(sem):
    return pltpu.CompilerParams(dimension_semantics=sem, vmem_limit_bytes=VMEM_LIMIT)


def _resident(shape):
    nd = len(shape)
    return pl.BlockSpec(shape, lambda *_: (0,) * nd, pipeline_mode=pl.Buffered(1))


def _ada_kernel(c_ref, w_ref, b_ref, o_ref):
    c = c_ref[...]
    ca = c * _sigmoid(c)
    o_ref[0] = jnp.dot(ca, w_ref[0], precision=lax.Precision.HIGHEST,
                       preferred_element_type=F32) + b_ref[0]


def _ada(c, w_ada, b_ada):
    depth, d, n = w_ada.shape
    bsz = c.shape[0]
    tn = 1152
    return pl.pallas_call(
        _ada_kernel,
        grid=(depth, n // tn),
        in_specs=[pl.BlockSpec((bsz, d), lambda l, j: (0, 0)),
                  pl.BlockSpec((1, d, tn), lambda l, j: (l, 0, j)),
                  pl.BlockSpec((1, 1, tn), lambda l, j: (l, 0, j))],
        out_specs=pl.BlockSpec((1, bsz, tn), lambda l, j: (l, 0, j)),
        out_shape=jax.ShapeDtypeStruct((depth, bsz, n), F32),
        compiler_params=_cparams(("parallel", "parallel")),
        name="ada_mod",
    )(c, w_ada, b_ada.reshape(depth, 1, n))


def _lb_kernel(l_ref, o_ref):
    x = l_ref[...]
    e = jnp.exp(x - jnp.max(x, axis=0, keepdims=True))
    p = e / jnp.sum(e, axis=0, keepdims=True)
    depth = x.shape[0]
    run = p[0:1]
    o_ref[0:1, :] = run - p[0:1]
    for l in range(1, depth):
        run = run + p[l:l + 1]
        o_ref[l:l + 1, :] = run - p[0:1]


def _lower_bounds(logits):
    return pl.pallas_call(_lb_kernel, out_shape=jax.ShapeDtypeStruct(logits.shape, F32),
                          name="hgrn_lower_bounds")(logits)


def _ffn_kernel(x_ref, mod_ref, g_ref, win_ref, wout_ref, o_ref, act_ref, *, res_w, ffc, sub):
    m = mod_ref[0]
    d_ff = wout_ref.shape[0]
    subs = [slice(s * sub, (s + 1) * sub) for s in range(x_ref.shape[0] // sub)]
    hb = [(_rms(x_ref[rs, :], g_ref[0:1]) * (1.0 + m[1:2]) + m[0:1]).astype(BF16) for rs in subs]
    for s, rs in enumerate(subs):
        for c in range(d_ff // ffc):
            a = _dot(hb[s], win_ref[:, c * ffc:(c + 1) * ffc])
            b = _dot(hb[s], win_ref[:, d_ff + c * ffc:d_ff + (c + 1) * ffc])
            act_ref[rs, c * ffc:(c + 1) * ffc] = (a * _sigmoid(a) * b).astype(BF16)
    ys = [_dot(act_ref[rs, :], wout_ref[...]) for rs in subs]
    for s, rs in enumerate(subs):
        o_ref[rs, :] = x_ref[rs, :] + res_w * m[2:3] * _rms(ys[s], g_ref[1:2])


def _ffn_sublayer(x, mod, g2, w_in, w_out, res_w, seq, tm=1024, ffc=MXU_WIDTH, sub=256):
    t, d = x.shape
    d_ff = w_out.shape[0]
    per_b = seq // tm
    return pl.pallas_call(
        functools.partial(_ffn_kernel, res_w=res_w, ffc=ffc, sub=sub),
        grid=(t // tm,),
        in_specs=[pl.BlockSpec((tm, d), lambda i: (i, 0)),
                  pl.BlockSpec((1, 3, d), lambda i: (i // per_b, 0, 0)),
                  _resident((2, d)),
                  _resident((d, 2 * d_ff)),
                  _resident((d_ff, d))],
        out_specs=pl.BlockSpec((tm, d), lambda i: (i, 0)),
        out_shape=jax.ShapeDtypeStruct((t, d), F32),
        scratch_shapes=[pltpu.VMEM((tm, d_ff), BF16)],
        compiler_params=_cparams(("parallel",)),
        name="ffn_sublayer",
    )(x, mod, g2, w_in, w_out)


def _inproj_kernel(x_ref, mod_ref, g_ref, w_ref, mix_ref, gate_ref, *, cc, sub):
    m = mod_ref[0]
    subs = [slice(s * sub, (s + 1) * sub) for s in range(x_ref.shape[0] // sub)]
    hb = [(_rms(x_ref[rs, :], g_ref[...]) * (1.0 + m[1:2]) + m[0:1]).astype(BF16) for rs in subs]
    for s, rs in enumerate(subs):
        for lo in range(0, IN_COLS, cc):
            hi = min(lo + cc, IN_COLS)
            z = _dot(hb[s], w_ref[:, lo:hi])
            mix_hi = min(hi, MIX_COLS)
            if lo < mix_hi:
                mix_ref[rs, lo:mix_hi] = z[:, :mix_hi - lo]
            if hi > MIX_COLS:
                g_lo = max(lo, MIX_COLS)
                gate_ref[rs, g_lo - MIX_COLS:hi - MIX_COLS] = _sigmoid(z[:, g_lo - lo:]).astype(BF16)


def _inproj(x, mod, g, w_in, seq, tm=512, cc=MXU_WIDTH, sub=256):
    t, d = x.shape
    per_b = seq // tm
    return pl.pallas_call(
        functools.partial(_inproj_kernel, cc=cc, sub=sub),
        grid=(t // tm,),
        in_specs=[pl.BlockSpec((tm, d), lambda i: (i, 0)),
                  pl.BlockSpec((1, 3, d), lambda i: (i // per_b, 0, 0)),
                  _resident((1, d)),
                  _resident((d, IN_COLS))],
        out_specs=[pl.BlockSpec((tm, MIX_COLS), lambda i: (i, 0)),
                   pl.BlockSpec((tm, GATE_COLS), lambda i: (i, 0))],
        out_shape=[jax.ShapeDtypeStruct((t, MIX_COLS), F32),
                   jax.ShapeDtypeStruct((t, GATE_COLS), BF16)],
        compiler_params=_cparams(("parallel",)),
        name="mix_inproj",
    )(x, mod, g, w_in)


def _hgrn_kernel(q_ref, f_ref, i_ref, g_ref, lb_ref, ng_ref, o_ref,
                 state_ref, qs_ref, ks_ref, bs_ref, oi_ref, *, ts):
    ck, kd, vd = A_CHUNK, A_KDIM, A_VDIM
    nc = ts // ck
    mid = ck // 2 - 1

    @pl.when(pl.program_id(2) == 0)
    def _():
        state_ref[...] = jnp.zeros_like(state_ref)

    def iota(shape, axis):
        return lax.broadcasted_iota(jnp.int32, shape, axis)

    tri2 = (iota((ck, 2 * ck), 0) >= iota((ck, 2 * ck), 1) % ck).astype(BF16)
    row1 = iota((ck, 1), 0)
    ng = ng_ref[...]
    ng2 = jnp.concatenate([ng, ng], axis=1)
    rows_of = lambda c: slice(c * ck, (c + 1) * ck)

    qraw = q_ref[0]
    fr = f_ref[0]
    lb = lb_ref[...]
    qs = qraw * _sigmoid(qraw)
    e = jnp.exp(-jnp.abs(fr))
    big = 1.0 / (1.0 + e)
    small = e * big
    pos = fr >= 0.0
    f = lb + (1.0 - lb) * jnp.where(pos, big, small)
    k = (1.0 - lb) * jnp.where(pos, small, big)
    lf_hi, lf_lo = _split_bf16(jnp.log(jnp.maximum(f, TINY)))
    b = jnp.concatenate(
        [_dot(tri2, jnp.concatenate([lf_hi[rows_of(c)], lf_lo[rows_of(c)]], axis=0)) for c in range(nc)],
        axis=0)
    worst = None
    for c in range(nc):
        b_mid = b[c * ck + mid:c * ck + mid + 1]
        b_end = b[(c + 1) * ck - 1:(c + 1) * ck]
        w = jnp.minimum(b_mid, b_end - b_mid)
        worst = w if worst is None else jnp.minimum(worst, w)
    safe = jnp.min(worst) > -HGRN_SAFE_SPAN

    @pl.when(safe)
    def _():
        key_head0 = iota((ck, 2 * kd), 1) < kd
        val_head0 = iota((ck, 2 * vd), 1) < vd
        causal2 = iota((ck, 2 * ck), 0) >= iota((ck, 2 * ck), 1) % ck
        r4 = iota((4 * vd, 2 * vd), 0) % (2 * vd)
        same_head_sum = ((r4 < vd) == (iota((4 * vd, 2 * vd), 1) < vd)).astype(BF16)

        def block_diag(x, head0):
            zero = jnp.zeros_like(x)
            return jnp.concatenate([jnp.where(head0, x, zero), jnp.where(head0, zero, x)], axis=0)

        sts = [state_ref[0:vd, 0:kd], state_ref[vd:, kd:]]
        zero_block = jnp.zeros((vd, kd), BF16)
        for first in range(0, nc, HGRN_BATCH):
            cs = range(first, min(first + HGRN_BATCH, nc))
            qa, ka, q_dec, k_end, end_decay, vb = {}, {}, {}, {}, {}, {}
            for c in cs:
                bc = b[rows_of(c)]
                b_mid = bc[mid:mid + 1]
                b_end = bc[ck - 1:ck]
                qa_f = qs[rows_of(c)] * jnp.exp(bc - b_mid)
                ka_f = k[rows_of(c)] * jnp.exp(b_mid - bc)
                qa[c] = qa_f.astype(BF16)
                ka[c] = block_diag(ka_f.astype(BF16), key_head0)
                q_dec[c] = (qa_f * jnp.exp(b_mid)).astype(BF16)
                k_end[c] = (ka_f * jnp.exp(b_end - b_mid)).astype(BF16)
                end_decay[c] = jnp.exp(b_end)
                vb[c] = i_ref[0, rows_of(c), :].astype(BF16)
            scores = {c: jnp.where(causal2, _dot_nt(qa[c], ka[c]), 0.0).astype(BF16) for c in cs}
            o_intra = {c: _dot(scores[c], block_diag(vb[c], val_head0)) for c in cs}
            update = {c: lax.dot_general(vb[c], k_end[c], (((0,), (0,)), ((), ())), preferred_element_type=F32)
                      for c in cs}
            state_before = {}
            for c in cs:
                state_before[c] = jnp.concatenate(
                    [jnp.concatenate([sts[0].astype(BF16), zero_block], axis=1),
                     jnp.concatenate([zero_block, sts[1].astype(BF16)], axis=1)], axis=0)
                sts = [end_decay[c][:, 0:kd] * sts[0] + update[c][0:vd, 0:kd],
                       end_decay[c][:, kd:] * sts[1] + update[c][vd:, kd:]]
            outs = {c: o_intra[c] + _dot_nt(q_dec[c], state_before[c]) for c in cs}
            squares = {}
            for c in cs:
                hi, lo = _split_bf16(outs[c] * outs[c])
                squares[c] = _dot(jnp.concatenate([hi, lo], axis=1), same_head_sum)
            for c in cs:
                gate = g_ref[0, rows_of(c), :]
                y = outs[c] * lax.rsqrt(squares[c] * (1.0 / vd) + EPS) * ng2
                o_ref[0, rows_of(c), :] = (y * (gate * _sigmoid(gate))).astype(o_ref.dtype)
        state_ref[0:vd, 0:kd] = sts[0]
        state_ref[vd:, kd:] = sts[1]

    @pl.when(jnp.logical_not(safe))
    def _():
        for j in range(2):
            ksl = slice(j * kd, (j + 1) * kd)
            vsl = slice(j * vd, (j + 1) * vd)
            qs_ref[j] = qs[:, ksl]
            ks_ref[j] = k[:, ksl]
            bs_ref[j] = b[:, ksl]

            def chunk(c, carry):
                r0 = pl.multiple_of(c * ck, ck)
                rows = pl.ds(r0, ck)
                bc = bs_ref[j, rows, :]
                qc = qs_ref[j, rows, :]
                kc = ks_ref[j, rows, :]
                v = i_ref[0, rows, vsl]
                vb = v.astype(BF16)

                def one_row(t, cr):
                    bt = bs_ref[j, pl.ds(r0 + t, 1), :]
                    qt = qs_ref[j, pl.ds(r0 + t, 1), :]
                    dec = jnp.exp(jnp.where(row1 <= t, bt - bc, NEG_BIG))
                    sc = jnp.sum(dec * qt * kc, axis=-1, keepdims=True)
                    oi_ref[pl.ds(t, 1), :] = jnp.sum(sc * v, axis=0, keepdims=True)
                    return cr

                lax.fori_loop(0, ck, one_row, 0)
                b_end = bc[ck - 1:ck]
                st = state_ref[vsl, ksl]
                o = oi_ref[...] + _dot_nt((qc * jnp.exp(bc)).astype(BF16), st.astype(BF16))
                k_end = (kc * jnp.exp(b_end - bc)).astype(BF16)
                state_ref[vsl, ksl] = jnp.exp(b_end) * st + lax.dot_general(
                    vb, k_end, (((0,), (0,)), ((), ())), preferred_element_type=F32)
                gate = g_ref[0, rows, vsl]
                o_ref[0, rows, vsl] = (_rms(o, ng) * (gate * _sigmoid(gate))).astype(o_ref.dtype)
                return carry

            lax.fori_loop(0, nc, chunk, 0)


def _hgrn(mix3, lb, ng, ts=2048):
    bsz, seq, _ = mix3.shape
    pairs = A_HEADS // 2
    kw, vw = 2 * A_KDIM, 2 * A_VDIM
    return pl.pallas_call(
        functools.partial(_hgrn_kernel, ts=ts),
        grid=(bsz, pairs, seq // ts),
        in_specs=[pl.BlockSpec((1, ts, kw), lambda b, p, s: (b, s, OFF_AQ * LANES // kw + p)),
                  pl.BlockSpec((1, ts, kw), lambda b, p, s: (b, s, OFF_AF * LANES // kw + p)),
                  pl.BlockSpec((1, ts, vw), lambda b, p, s: (b, s, OFF_AI + p)),
                  pl.BlockSpec((1, ts, vw), lambda b, p, s: (b, s, OFF_AG + p)),
                  pl.BlockSpec((1, kw), lambda b, p, s: (0, p)),
                  pl.BlockSpec((1, A_VDIM), lambda b, p, s: (0, 0))],
        out_specs=pl.BlockSpec((1, ts, vw), lambda b, p, s: (b, s, p)),
        out_shape=jax.ShapeDtypeStruct((bsz, seq, A_V), BF16),
        scratch_shapes=[pltpu.VMEM((2 * A_VDIM, 2 * A_KDIM), F32),
                        pltpu.VMEM((2, ts, A_KDIM), F32),
                        pltpu.VMEM((2, ts, A_KDIM), F32),
                        pltpu.VMEM((2, ts, A_KDIM), F32),
                        pltpu.VMEM((A_CHUNK, A_VDIM), F32)],
        compiler_params=_cparams(("parallel", "parallel", "arbitrary")),
        name="hgrn2",
    )(mix3, mix3, mix3, mix3, lb, ng)


def _sb_kernel(q_ref, k_ref, v_ref, o_ref, run_ref, acc_ref):
    blk, d = B_BLOCK, B_HDIM
    seq = q_ref.shape[1]
    neg_scale = -(d ** -0.5) * LOG2_E

    def iota(shape, axis):
        return lax.broadcasted_iota(jnp.int32, shape, axis)

    strictly_before = iota((blk, 2 * blk), 1) % blk < iota((blk, 2 * blk), 0)
    r2, c2 = iota((2 * blk, 2 * blk), 0), iota((2 * blk, 2 * blk), 1)
    later_and_ones = jnp.logical_or(r2 % blk > c2, c2 >= blk).astype(BF16)
    head0_lanes = iota((blk, 2 * d), 1) < d

    def load_neg_q(q0):
        return (q_ref[0, pl.ds(q0, blk), :] * neg_scale).astype(BF16)

    def block_diag(x):
        zero = jnp.zeros_like(x)
        return jnp.concatenate([jnp.where(head0_lanes, x, zero), jnp.where(head0_lanes, zero, x)], axis=0)

    def logs(neg_z):
        log_1m = jnp.minimum(neg_z, 0.0) - jnp.log(1.0 + jnp.exp2(-jnp.abs(neg_z))) * LOG2_E
        return log_1m, log_1m - neg_z

    def later_sums(log_1m):
        hi, lo = _split_bf16(log_1m)
        per_head = [_dot(jnp.concatenate([hi[:, h * blk:(h + 1) * blk], lo[:, h * blk:(h + 1) * blk]], axis=1),
                         later_and_ones) for h in range(2)]
        within = jnp.concatenate([per_head[0][:, :blk], per_head[1][:, :blk]], axis=1)
        total = jnp.concatenate([per_head[0][:, blk:], per_head[1][:, blk:]], axis=1)
        return within, total

    def qgroup(qg, carry0, edge):
        qbs = [qg * SB_QBLOCKS_PER_ITER + i for i in range(SB_QBLOCKS_PER_ITER)]
        q0s = [pl.multiple_of(qb * blk, blk) for qb in qbs]
        units = [(i, step) for i in range(SB_QBLOCKS_PER_ITER) for step in range(SB_FULL_BLOCKS)]
        keeps, vals, neg_z = {}, {}, {}
        for i, step in units:
            kb = qbs[i] - step
            k0 = pl.multiple_of(jnp.maximum(kb, 0) * blk, blk)
            keeps[i, step] = strictly_before if step == 0 else (kb >= 0 if edge else None)
            vals[i, step] = block_diag(v_ref[0, pl.ds(k0, blk), :].astype(BF16))
            neg_z[i, step] = _dot_nt(load_neg_q(q0s[i]), block_diag(k_ref[0, pl.ds(k0, blk), :].astype(BF16)))
        log_bs, sums = {}, {}
        for u in units:
            log_1m, log_bs[u] = logs(neg_z[u])
            sums[u] = log_1m if keeps[u] is None else jnp.where(keeps[u], log_1m, 0.0)
        for u in units:
            sums[u] = later_sums(sums[u])
        weights, tops = {}, {}
        for i in range(SB_QBLOCKS_PER_ITER):
            run = jnp.zeros((blk, 2 * blk), F32)
            for step in range(SB_FULL_BLOCKS):
                u = (i, step)
                within, total = sums[u]
                a = jnp.exp2(log_bs[u] + within + run)
                weights[u] = (a if keeps[u] is None else jnp.where(keeps[u], a, 0.0)).astype(BF16)
                run = run + total
            run_ref[i] = run
            tops[i] = jnp.max(run)
        for i in range(SB_QBLOCKS_PER_ITER):
            acc = _dot(weights[i, 0], vals[i, 0])
            for step in range(1, SB_FULL_BLOCKS):
                acc = acc + _dot(weights[i, step], vals[i, step])
            acc_ref[i] = acc

        def cond(st):
            kb, top = st
            return jnp.logical_and(kb >= 0, top > SB_SKIP_LOG * LOG2_E)

        for i in range(SB_QBLOCKS_PER_ITER):
            def body(st, i=i):
                kb, _ = st
                k0 = pl.multiple_of(kb * blk, blk)
                log_1m, log_b = logs(_dot_nt(load_neg_q(q0s[i]),
                                             block_diag(k_ref[0, pl.ds(k0, blk), :].astype(BF16))))
                within, total = later_sums(log_1m)
                run = run_ref[i]
                a = jnp.exp2(log_b + within + run)
                acc_ref[i] = acc_ref[i] + _dot(a.astype(BF16), block_diag(v_ref[0, pl.ds(k0, blk), :].astype(BF16)))
                run = run + total
                run_ref[i] = run
                return kb - 1, jnp.max(run)

            lax.while_loop(cond, body, (qbs[i] - SB_FULL_BLOCKS, tops[i]))
            o_ref[0, pl.ds(q0s[i], blk), :] = acc_ref[i].astype(o_ref.dtype)
        return carry0

    first_full = -(-(SB_FULL_BLOCKS - 1) // SB_QBLOCKS_PER_ITER)
    for qg in range(first_full):
        qgroup(qg, 0, True)
    lax.fori_loop(first_full, seq // (blk * SB_QBLOCKS_PER_ITER), functools.partial(qgroup, edge=False), 0)


def _stick_breaking(mix3):
    bsz, seq, _ = mix3.shape
    pairs = B_HEADS // 2
    w = 2 * B_HDIM
    return pl.pallas_call(
        _sb_kernel,
        grid=(bsz, pairs),
        in_specs=[pl.BlockSpec((1, seq, w), lambda b, p: (b, 0, OFF_BQ + p)),
                  pl.BlockSpec((1, seq, w), lambda b, p: (b, 0, OFF_BK + p)),
                  pl.BlockSpec((1, seq, w), lambda b, p: (b, 0, OFF_BV + p))],
        out_specs=pl.BlockSpec((1, seq, w), lambda b, p: (b, 0, p)),
        out_shape=jax.ShapeDtypeStruct((bsz, seq, B_W), BF16),
        scratch_shapes=[pltpu.VMEM((SB_QBLOCKS_PER_ITER, B_BLOCK, 2 * B_BLOCK), F32),
                        pltpu.VMEM((SB_QBLOCKS_PER_ITER, B_BLOCK, 2 * B_HDIM), F32)],
        compiler_params=_cparams(("parallel", "parallel")),
        name="stick_breaking",
    )(mix3, mix3, mix3)


def _dil_kernel(*refs, tile, slopes):
    blk, d = C_BLOCK, C_HDIM
    n_groups = len(C_GROUPS)
    o_ref, os_ref, ls_ref, stage_ref = refs[5 * n_groups:]
    pair = pl.program_id(1)
    first_tile = pl.program_id(2) == 0
    scale = d ** -0.5
    qi = lax.broadcasted_iota(jnp.int32, (blk, 4 * blk), 0)
    lane4 = lax.broadcasted_iota(jnp.int32, (blk, 4 * blk), 1)
    kj = lane4 % (2 * blk)
    head0_keys = lane4 < 2 * blk
    delta = qi + blk - kj
    head0_out = lax.broadcasted_iota(jnp.int32, (blk, 2 * d), 1) < d
    all_ones = jnp.ones((2 * blk, 2 * d), BF16)

    for g, (window, dil) in enumerate(C_GROUPS):
        q_ref, k_ref, kp_ref, v_ref, vp_ref = group_refs = refs[5 * g:5 * g + 5]
        inner = min(dil, DIL_MAX_STRIDE)
        outer = dil // inner
        if outer > 1:
            for idx, ref in enumerate(group_refs):
                n = ref.shape[1] // inner
                for r in range(inner):
                    stage_ref[idx, r * n:(r + 1) * n, :] = ref[0, pl.ds(r, n, stride=inner), :]
        window_ok = jnp.logical_and(delta >= 0, delta <= window // dil)
        window_ok_first = jnp.logical_and(window_ok, jnp.logical_or(kj >= blk, jnp.logical_not(first_tile)))
        slope0 = jnp.where(pair == 0, jnp.float32(slopes[g][0]), jnp.float32(slopes[g][2]))
        slope1 = jnp.where(pair == 0, jnp.float32(slopes[g][1]), jnp.float32(slopes[g][3]))
        alibi = jnp.where(head0_keys, slope0, slope1) * (delta * dil).astype(F32)

        def strided(ref, rho, block):
            if outer > 1:
                idx = [r is ref for r in group_refs].index(True)
                start = (rho % inner) * (ref.shape[1] // inner) + rho // inner + outer * block * blk
                return stage_ref[idx, pl.ds(start, blk, stride=outer), :]
            start = rho + block * blk * dil
            return ref[0, pl.ds(start, blk, stride=dil), :] if dil > 1 else ref[0, pl.ds(start, blk), :]

        blocks = [(rho, i) for rho in range(dil) for i in range(tile // (dil * blk))]
        for at in range(0, len(blocks), DIL_BATCH):
            batch = blocks[at:at + DIL_BATCH]
            scores, values = {}, {}
            for rho, i in batch:
                k2 = jnp.concatenate([strided(kp_ref, rho, 0) if i == 0 else strided(k_ref, rho, i - 1),
                                      strided(k_ref, rho, i)], axis=0).astype(BF16)
                v2 = jnp.concatenate([strided(vp_ref, rho, 0) if i == 0 else strided(v_ref, rho, i - 1),
                                      strided(v_ref, rho, i)], axis=0).astype(BF16)
                q = (strided(q_ref, rho, i) * scale).astype(BF16)
                zero = jnp.zeros_like(q)
                scores[rho, i] = jnp.concatenate([_dot_nt(jnp.where(head0_out, q, zero), k2),
                                                  _dot_nt(jnp.where(head0_out, zero, q), k2)], axis=1)
                values[rho, i] = jnp.concatenate([v2, all_ones], axis=1)
            probs, tops = {}, {}
            for rho, i in batch:
                valid = window_ok_first if i == 0 else window_ok
                s = jnp.where(valid, scores[rho, i] - alibi, NEG_BIG)
                mx0 = jnp.max(s[:, :2 * blk], axis=-1, keepdims=True)
                mx1 = jnp.max(s[:, 2 * blk:], axis=-1, keepdims=True)
                probs[rho, i] = jnp.exp(s - jnp.where(head0_keys, mx0, mx1)).astype(BF16)
                tops[rho, i] = jnp.where(head0_out, mx0, mx1)
            for rho, i in batch:
                r0 = _dot(probs[rho, i][:, :2 * blk], values[rho, i])
                r1 = _dot(probs[rho, i][:, 2 * blk:], values[rho, i])
                den = jnp.where(head0_out, r0[:, 2 * d:], r1[:, 2 * d:])
                start = rho + i * blk * dil
                rows = pl.ds(start, blk, stride=dil) if dil > 1 else pl.ds(start, blk)
                os_ref[g, rows, :] = jnp.where(head0_out, r0[:, :2 * d], r1[:, :2 * d]) / den
                ls_ref[g, rows, :] = tops[rho, i] + jnp.log(den)
    lses = [ls_ref[g] for g in range(n_groups)]
    top = functools.reduce(jnp.maximum, lses)
    wts = [jnp.exp(l - top) for l in lses]
    num = functools.reduce(lambda a, b: a + b, [wts[g] * os_ref[g] for g in range(n_groups)])
    o_ref[0] = (num / functools.reduce(lambda a, b: a + b, wts)).astype(o_ref.dtype)


def _dilated(mix3):
    bsz, seq, _ = mix3.shape
    pairs = C_HEADS_PER_GROUP // 2
    w = 2 * C_HDIM
    tile = C_BLOCK * max(dil for _, dil in C_GROUPS)
    all_slopes = _alibi_slopes(C_HEADS)
    slopes = tuple(tuple(all_slopes[g * C_HEADS_PER_GROUP:(g + 1) * C_HEADS_PER_GROUP])
                   for g in range(len(C_GROUPS)))
    in_specs = []
    for g, (_, dil) in enumerate(C_GROUPS):
        back = C_BLOCK * dil
        per = tile // back

        def cur(off, g=g):
            return pl.BlockSpec((1, tile, w), lambda b, p, n: (b, n, off + pairs * g + p))

        def prev(off, g=g, back=back, per=per):
            return pl.BlockSpec((1, back, w), lambda b, p, n: (b, jnp.maximum(n * per - 1, 0), off + pairs * g + p))

        in_specs += [cur(OFF_CQ), cur(OFF_CK), prev(OFF_CK), cur(OFF_CV), prev(OFF_CV)]
    return pl.pallas_call(
        functools.partial(_dil_kernel, tile=tile, slopes=slopes),
        grid=(bsz, pairs, seq // tile),
        in_specs=in_specs,
        out_specs=pl.BlockSpec((1, tile, w), lambda b, p, n: (b, n, p)),
        out_shape=jax.ShapeDtypeStruct((bsz, seq, C_OUT), BF16),
        scratch_shapes=[pltpu.VMEM((len(C_GROUPS), tile, w), F32),
                        pltpu.VMEM((len(C_GROUPS), tile, w), F32),
                        pltpu.VMEM((5, tile, w), F32)],
        compiler_params=_cparams(("parallel", "parallel", "arbitrary")),
        name="dilated",
    )(*([mix3] * (5 * len(C_GROUPS))))


def _mixout_kernel(x_ref, mod_ref, g_ref, ya_ref, yb_ref, yc_ref, gate_ref, wa_ref, wb_ref, wc_ref, wo_ref,
                   out_ref, *, sub):
    d = D_MODEL
    subs = [slice(s * sub, (s + 1) * sub) for s in range(x_ref.shape[0] // sub)]
    merged = []
    for rs in subs:
        m = gate_ref[rs, 0:d].astype(F32) * _dot(ya_ref[rs, :], wa_ref[...])
        m = m + gate_ref[rs, d:2 * d].astype(F32) * _dot(yb_ref[rs, :], wb_ref[...])
        m = m + gate_ref[rs, 2 * d:3 * d].astype(F32) * _dot(yc_ref[rs, :], wc_ref[...])
        merged.append(m.astype(BF16))
    ys = [_dot(m, wo_ref[...]) for m in merged]
    for rs, y in zip(subs, ys):
        out_ref[rs, :] = x_ref[rs, :] + mod_ref[0][2:3] * _rms(y, g_ref[...])


def _mixout(x, mod, g, ya, yb, yc, gates, wa, wb, wc, wo, seq, tm=1024, sub=512):
    t, d = x.shape
    per_b = seq // tm

    def tok(wd):
        return pl.BlockSpec((tm, wd), lambda i: (i, 0))

    return pl.pallas_call(
        functools.partial(_mixout_kernel, sub=sub),
        grid=(t // tm,),
        in_specs=[tok(d), pl.BlockSpec((1, 3, d), lambda i: (i // per_b, 0, 0)), _resident((1, d)),
                  tok(A_V), tok(B_W), tok(C_OUT), tok(GATE_COLS),
                  _resident(wa.shape), _resident(wb.shape), _resident(wc.shape), _resident(wo.shape)],
        out_specs=tok(d),
        out_shape=jax.ShapeDtypeStruct((t, d), F32),
        compiler_params=_cparams(("parallel",)),
        name="mix_out",
    )(x, mod, g, ya, yb, yc, gates, wa, wb, wc, wo)


def kernel(x, c, w_ada, b_ada, norm_g, ffn1_w_in, ffn1_w_out, w_in, hgrn_lb_logits, hgrn_norm_g,
           w_branch_a, w_branch_b, w_branch_c, w_out, ffn2_w_in, ffn2_w_out):
    bsz, seq, d = x.shape
    depth = w_ada.shape[0]
    lb_all = _lower_bounds(hgrn_lb_logits.astype(F32))
    mod = _ada(c, w_ada, b_ada).reshape(depth, bsz, 3, 3, d)
    xt = x.reshape(bsz * seq, d)
    for l in range(depth):
        xt = _ffn_sublayer(xt, mod[l, :, 0], norm_g[l, 0:2], ffn1_w_in[l].astype(BF16),
                           ffn1_w_out[l].astype(BF16), 0.5, seq)
        mix, gates = _inproj(xt, mod[l, :, 1], norm_g[l, 2:3], w_in[l].astype(BF16), seq)
        mix3 = mix.reshape(bsz, seq, MIX_COLS)
        ya = _hgrn(mix3, lb_all[l:l + 1], hgrn_norm_g[l:l + 1]).reshape(bsz * seq, A_V)
        yb = _stick_breaking(mix3).reshape(bsz * seq, B_W)
        yc = _dilated(mix3).reshape(bsz * seq, C_OUT)
        xt = _mixout(xt, mod[l, :, 1], norm_g[l, 3:4], ya, yb, yc, gates,
                     w_branch_a[l].astype(BF16), w_branch_b[l].astype(BF16), w_branch_c[l].astype(BF16),
                     w_out[l].astype(BF16), seq)
        xt = _ffn_sublayer(xt, mod[l, :, 2], norm_g[l, 4:6], ffn2_w_in[l].astype(BF16),
                           ffn2_w_out[l].astype(BF16), 0.5, seq)
    return xt.reshape(bsz, seq, d)
```

```python
import functools
import math

import jax
import jax.numpy as jnp
from jax import lax
from jax.experimental import pallas as pl
from jax.experimental.pallas import tpu as pltpu

D_MODEL = 1024
DEPTH = 4
A_HEADS = 6
A_KDIM = 128
A_VDIM = 64
A_CHUNK = 64
B_HEADS = 6
B_HDIM = 64
B_BLOCK = 128
C_GROUPS = ((128, 1), (512, 4), (2048, 16))
C_HEADS_PER_GROUP = 4
C_HEADS = C_HEADS_PER_GROUP * len(C_GROUPS)
C_HDIM = 64
C_BLOCK = 128
D_FF = 2816
N_BRANCH = 3
EPS = 1e-6
NEG_BIG = -1e30
TINY = 1e-30

A_QK = A_HEADS * A_KDIM
A_V = A_HEADS * A_VDIM
B_W = B_HEADS * B_HDIM
C_W = C_HEADS * C_HDIM
C_OUT = C_HEADS_PER_GROUP * C_HDIM
MIX_COLS = 2 * A_QK + 2 * A_V + 3 * B_W + 3 * C_W
GATE_COLS = N_BRANCH * D_MODEL
IN_COLS = MIX_COLS + GATE_COLS

LANES = 128
MXU_WIDTH = 256
OFF_AQ = 0
OFF_AF = A_QK // LANES
OFF_AI = 2 * A_QK // LANES
OFF_AG = (2 * A_QK + A_V) // LANES
OFF_BQ = (2 * A_QK + 2 * A_V) // LANES
OFF_BK = OFF_BQ + B_W // LANES
OFF_BV = OFF_BK + B_W // LANES
OFF_CQ = OFF_BV + B_W // LANES
OFF_CK = OFF_CQ + C_W // LANES
OFF_CV = OFF_CK + C_W // LANES
MIX_BLOCKS = MIX_COLS // LANES

VMEM_LIMIT = 56 * 1024 * 1024
ADA_COL_BLOCKS = 8
SB_SKIP_LOG = -105.0
LOG2_E = 1.4426950408889634
SB_FULL_BLOCKS = 3
SB_QBLOCKS_PER_ITER = 8
HGRN_BATCH = 32
DIL_MAX_STRIDE = 4
DIL_BATCH = 4
HGRN_SAFE_SPAN = 80.0

F32 = jnp.float32
BF16 = jnp.bfloat16


def _alibi_slopes(n):
    def pow2_slopes(m):
        start = 2.0 ** (-8.0 / m)
        return [start ** (i + 1) for i in range(m)]
    if math.log2(n).is_integer():
        s = pow2_slopes(n)
    else:
        c = 2 ** int(math.floor(math.log2(n)))
        s = pow2_slopes(c) + pow2_slopes(2 * c)[0::2][: n - c]
    return sorted(s, reverse=True)


def _rms(x, g):
    return x * lax.rsqrt(jnp.mean(x * x, axis=-1, keepdims=True) + EPS) * g


def _sigmoid(x):
    return 1.0 / (1.0 + jnp.exp(-x))


def _dot(a, b):
    return jnp.dot(a, b, preferred_element_type=F32)


def _dot_nt(a, b):
    return lax.dot_general(a, b, (((1,), (1,)), ((), ())), preferred_element_type=F32)


def _split_bf16(x):
    hi = x.astype(BF16)
    lo = (x - hi.astype(F32)).astype(BF16)
    return hi, lo


def _cparams(sem):
    return pltpu.CompilerParams(dimension_semantics=sem, vmem_limit_bytes=VMEM_LIMIT)


def _resident(shape):
    nd = len(shape)
    return pl.BlockSpec(shape, lambda *_: (0,) * nd, pipeline_mode=pl.Buffered(1))


def _ada_kernel(c_ref, w_ref, b_ref, o_ref):
    c = c_ref[...]
    ca = c * _sigmoid(c)
    o_ref[0] = jnp.dot(ca, w_ref[0], precision=lax.Precision.HIGHEST,
                       preferred_element_type=F32) + b_ref[0]


def _ada(c, w_ada, b_ada):
    depth, d, n = w_ada.shape
    bsz = c.shape[0]
    tn = n // ADA_COL_BLOCKS
    return pl.pallas_call(
        _ada_kernel,
        grid=(depth, n // tn),
        in_specs=[pl.BlockSpec((bsz, d), lambda l, j: (0, 0)),
                  pl.BlockSpec((1, d, tn), lambda l, j: (l, 0, j)),
                  pl.BlockSpec((1, 1, tn), lambda l, j: (l, 0, j))],
        out_specs=pl.BlockSpec((1, bsz, tn), lambda l, j: (l, 0, j)),
        out_shape=jax.ShapeDtypeStruct((depth, bsz, n), F32),
        compiler_params=_cparams(("parallel", "parallel")),
        name="ada_mod",
    )(c, w_ada, b_ada.reshape(depth, 1, n))


def _lb_kernel(l_ref, o_ref):
    x = l_ref[...]
    e = jnp.exp(x - jnp.max(x, axis=0, keepdims=True))
    p = e / jnp.sum(e, axis=0, keepdims=True)
    depth = x.shape[0]
    run = p[0:1]
    o_ref[0:1, :] = run - p[0:1]
    for l in range(1, depth):
        run = run + p[l:l + 1]
        o_ref[l:l + 1, :] = run - p[0:1]


def _lower_bounds(logits):
    return pl.pallas_call(_lb_kernel, out_shape=jax.ShapeDtypeStruct(logits.shape, F32),
                          name="hgrn_lower_bounds")(logits)


def _ffn_kernel(x_ref, mod_ref, g_ref, win_ref, wout_ref, o_ref, act_ref, *, res_w, ffc, sub):
    m = mod_ref[0]
    d_ff = wout_ref.shape[0]
    subs = [slice(s * sub, (s + 1) * sub) for s in range(x_ref.shape[0] // sub)]
    hb = [(_rms(x_ref[rs, :], g_ref[0:1]) * (1.0 + m[1:2]) + m[0:1]).astype(BF16) for rs in subs]
    for s, rs in enumerate(subs):
        for c in range(d_ff // ffc):
            a = _dot(hb[s], win_ref[:, c * ffc:(c + 1) * ffc])
            b = _dot(hb[s], win_ref[:, d_ff + c * ffc:d_ff + (c + 1) * ffc])
            act_ref[rs, c * ffc:(c + 1) * ffc] = (a * _sigmoid(a) * b).astype(BF16)
    ys = [_dot(act_ref[rs, :], wout_ref[...]) for rs in subs]
    for s, rs in enumerate(subs):
        o_ref[rs, :] = x_ref[rs, :] + res_w * m[2:3] * _rms(ys[s], g_ref[1:2])


def _ffn_sublayer(x, mod, g2, w_in, w_out, res_w, seq, tm=1024, ffc=MXU_WIDTH, sub=256):
    t, d = x.shape
    d_ff = w_out.shape[0]
    per_b = seq // tm
    return pl.pallas_call(
        functools.partial(_ffn_kernel, res_w=res_w, ffc=ffc, sub=sub),
        grid=(t // tm,),
        in_specs=[pl.BlockSpec((tm, d), lambda i: (i, 0)),
                  pl.BlockSpec((1, 3, d), lambda i: (i // per_b, 0, 0)),
                  _resident((2, d)),
                  _resident((d, 2 * d_ff)),
                  _resident((d_ff, d))],
        out_specs=pl.BlockSpec((tm, d), lambda i: (i, 0)),
        out_shape=jax.ShapeDtypeStruct((t, d), F32),
        scratch_shapes=[pltpu.VMEM((tm, d_ff), BF16)],
        compiler_params=_cparams(("parallel",)),
        name="ffn_sublayer",
    )(x, mod, g2, w_in, w_out)


def _inproj_kernel(x_ref, mod_ref, g_ref, w_ref, mix_ref, gate_ref, *, cc, sub):
    m = mod_ref[0]
    subs = [slice(s * sub, (s + 1) * sub) for s in range(x_ref.shape[0] // sub)]
    hb = [(_rms(x_ref[rs, :], g_ref[...]) * (1.0 + m[1:2]) + m[0:1]).astype(BF16) for rs in subs]
    for s, rs in enumerate(subs):
        for lo in range(0, IN_COLS, cc):
            hi = min(lo + cc, IN_COLS)
            z = _dot(hb[s], w_ref[:, lo:hi])
            mix_hi = min(hi, MIX_COLS)
            if lo < mix_hi:
                mix_ref[rs, lo:mix_hi] = z[:, :mix_hi - lo]
            if hi > MIX_COLS:
                g_lo = max(lo, MIX_COLS)
                gate_ref[rs, g_lo - MIX_COLS:hi - MIX_COLS] = _sigmoid(z[:, g_lo - lo:]).astype(BF16)


def _inproj(x, mod, g, w_in, seq, tm=512, cc=MXU_WIDTH, sub=256):
    t, d = x.shape
    per_b = seq // tm
    return pl.pallas_call(
        functools.partial(_inproj_kernel, cc=cc, sub=sub),
        grid=(t // tm,),
        in_specs=[pl.BlockSpec((tm, d), lambda i: (i, 0)),
                  pl.BlockSpec((1, 3, d), lambda i: (i // per_b, 0, 0)),
                  _resident((1, d)),
                  _resident((d, IN_COLS))],
        out_specs=[pl.BlockSpec((tm, MIX_COLS), lambda i: (i, 0)),
                   pl.BlockSpec((tm, GATE_COLS), lambda i: (i, 0))],
        out_shape=[jax.ShapeDtypeStruct((t, MIX_COLS), F32),
                   jax.ShapeDtypeStruct((t, GATE_COLS), BF16)],
        compiler_params=_cparams(("parallel",)),
        name="mix_inproj",
    )(x, mod, g, w_in)


def _hgrn_kernel(q_ref, f_ref, i_ref, g_ref, lb_ref, ng_ref, o_ref,
                 state_ref, qs_ref, ks_ref, bs_ref, oi_ref, *, ts):
    ck, kd, vd = A_CHUNK, A_KDIM, A_VDIM
    nc = ts // ck
    mid = ck // 2 - 1

    @pl.when(pl.program_id(2) == 0)
    def _():
        state_ref[...] = jnp.zeros_like(state_ref)

    def iota(shape, axis):
        return lax.broadcasted_iota(jnp.int32, shape, axis)

    tri2 = (iota((ck, 2 * ck), 0) >= iota((ck, 2 * ck), 1) % ck).astype(BF16)
    row1 = iota((ck, 1), 0)
    ng = ng_ref[...]
    ng2 = jnp.concatenate([ng, ng], axis=1)
    rows_of = lambda c: slice(c * ck, (c + 1) * ck)

    qraw = q_ref[0]
    fr = f_ref[0]
    lb = lb_ref[...]
    qs = qraw * _sigmoid(qraw)
    e = jnp.exp(-jnp.abs(fr))
    big = 1.0 / (1.0 + e)
    small = e * big
    pos = fr >= 0.0
    f = lb + (1.0 - lb) * jnp.where(pos, big, small)
    k = (1.0 - lb) * jnp.where(pos, small, big)
    lf_hi, lf_lo = _split_bf16(jnp.log(jnp.maximum(f, TINY)))
    b = jnp.concatenate(
        [_dot(tri2, jnp.concatenate([lf_hi[rows_of(c)], lf_lo[rows_of(c)]], axis=0)) for c in range(nc)],
        axis=0)
    worst = None
    for c in range(nc):
        b_mid = b[c * ck + mid:c * ck + mid + 1]
        b_end = b[(c + 1) * ck - 1:(c + 1) * ck]
        w = jnp.minimum(b_mid, b_end - b_mid)
        worst = w if worst is None else jnp.minimum(worst, w)
    safe = jnp.min(worst) > -HGRN_SAFE_SPAN

    @pl.when(safe)
    def _():
        key_head0 = iota((ck, 2 * kd), 1) < kd
        val_head0 = iota((ck, 2 * vd), 1) < vd
        causal2 = iota((ck, 2 * ck), 0) >= iota((ck, 2 * ck), 1) % ck
        r4 = iota((4 * vd, 2 * vd), 0) % (2 * vd)
        same_head_sum = ((r4 < vd) == (iota((4 * vd, 2 * vd), 1) < vd)).astype(BF16)

        def block_diag(x, head0):
            zero = jnp.zeros_like(x)
            return jnp.concatenate([jnp.where(head0, x, zero), jnp.where(head0, zero, x)], axis=0)

        sts = [state_ref[0:vd, 0:kd], state_ref[vd:, kd:]]
        zero_block = jnp.zeros((vd, kd), BF16)
        for first in range(0, nc, HGRN_BATCH):
            cs = range(first, min(first + HGRN_BATCH, nc))
            qa, ka, q_dec, k_end, end_decay, vb = {}, {}, {}, {}, {}, {}
            for c in cs:
                bc = b[rows_of(c)]
                b_mid = bc[mid:mid + 1]
                b_end = bc[ck - 1:ck]
                qa_f = qs[rows_of(c)] * jnp.exp(bc - b_mid)
                ka_f = k[rows_of(c)] * jnp.exp(b_mid - bc)
                qa[c] = qa_f.astype(BF16)
                ka[c] = block_diag(ka_f.astype(BF16), key_head0)
                q_dec[c] = (qa_f * jnp.exp(b_mid)).astype(BF16)
                k_end[c] = (ka_f * jnp.exp(b_end - b_mid)).astype(BF16)
                end_decay[c] = jnp.exp(b_end)
                vb[c] = i_ref[0, rows_of(c), :].astype(BF16)
            scores = {c: jnp.where(causal2, _dot_nt(qa[c], ka[c]), 0.0).astype(BF16) for c in cs}
            o_intra = {c: _dot(scores[c], block_diag(vb[c], val_head0)) for c in cs}
            update = {c: lax.dot_general(vb[c], k_end[c], (((0,), (0,)), ((), ())), preferred_element_type=F32)
                      for c in cs}
            state_before = {}
            for c in cs:
                state_before[c] = jnp.concatenate(
                    [jnp.concatenate([sts[0].astype(BF16), zero_block], axis=1),
                     jnp.concatenate([zero_block, sts[1].astype(BF16)], axis=1)], axis=0)
                sts = [end_decay[c][:, 0:kd] * sts[0] + update[c][0:vd, 0:kd],
                       end_decay[c][:, kd:] * sts[1] + update[c][vd:, kd:]]
            outs = {c: o_intra[c] + _dot_nt(q_dec[c], state_before[c]) for c in cs}
            squares = {}
            for c in cs:
                hi, lo = _split_bf16(outs[c] * outs[c])
                squares[c] = _dot(jnp.concatenate([hi, lo], axis=1), same_head_sum)
            for c in cs:
                gate = g_ref[0, rows_of(c), :]
                y = outs[c] * lax.rsqrt(squares[c] * (1.0 / vd) + EPS) * ng2
                o_ref[0, rows_of(c), :] = (y * (gate * _sigmoid(gate))).astype(o_ref.dtype)
        state_ref[0:vd, 0:kd] = sts[0]
        state_ref[vd:, kd:] = sts[1]

    @pl.when(jnp.logical_not(safe))
    def _():
        for j in range(2):
            ksl = slice(j * kd, (j + 1) * kd)
            vsl = slice(j * vd, (j + 1) * vd)
            qs_ref[j] = qs[:, ksl]
            ks_ref[j] = k[:, ksl]
            bs_ref[j] = b[:, ksl]

            def chunk(c, carry):
                r0 = pl.multiple_of(c * ck, ck)
                rows = pl.ds(r0, ck)
                bc = bs_ref[j, rows, :]
                qc = qs_ref[j, rows, :]
                kc = ks_ref[j, rows, :]
                v = i_ref[0, rows, vsl]
                vb = v.astype(BF16)

                def one_row(t, cr):
                    bt = bs_ref[j, pl.ds(r0 + t, 1), :]
                    qt = qs_ref[j, pl.ds(r0 + t, 1), :]
                    dec = jnp.exp(jnp.where(row1 <= t, bt - bc, NEG_BIG))
                    sc = jnp.sum(dec * qt * kc, axis=-1, keepdims=True)
                    oi_ref[pl.ds(t, 1), :] = jnp.sum(sc * v, axis=0, keepdims=True)
                    return cr

                lax.fori_loop(0, ck, one_row, 0)
                b_end = bc[ck - 1:ck]
                st = state_ref[vsl, ksl]
                o = oi_ref[...] + _dot_nt((qc * jnp.exp(bc)).astype(BF16), st.astype(BF16))
                k_end = (kc * jnp.exp(b_end - bc)).astype(BF16)
                state_ref[vsl, ksl] = jnp.exp(b_end) * st + lax.dot_general(
                    vb, k_end, (((0,), (0,)), ((), ())), preferred_element_type=F32)
                gate = g_ref[0, rows, vsl]
                o_ref[0, rows, vsl] = (_rms(o, ng) * (gate * _sigmoid(gate))).astype(o_ref.dtype)
                return carry

            lax.fori_loop(0, nc, chunk, 0)


def _hgrn(mix3, lb, ng, ts=2048):
    bsz, seq, _ = mix3.shape
    pairs = A_HEADS // 2
    kw, vw = 2 * A_KDIM, 2 * A_VDIM
    return pl.pallas_call(
        functools.partial(_hgrn_kernel, ts=ts),
        grid=(bsz, pairs, seq // ts),
        in_specs=[pl.BlockSpec((1, ts, kw), lambda b, p, s: (b, s, OFF_AQ * LANES // kw + p)),
                  pl.BlockSpec((1, ts, kw), lambda b, p, s: (b, s, OFF_AF * LANES // kw + p)),
                  pl.BlockSpec((1, ts, vw), lambda b, p, s: (b, s, OFF_AI + p)),
                  pl.BlockSpec((1, ts, vw), lambda b, p, s: (b, s, OFF_AG + p)),
                  pl.BlockSpec((1, kw), lambda b, p, s: (0, p)),
                  pl.BlockSpec((1, A_VDIM), lambda b, p, s: (0, 0))],
        out_specs=pl.BlockSpec((1, ts, vw), lambda b, p, s: (b, s, p)),
        out_shape=jax.ShapeDtypeStruct((bsz, seq, A_V), BF16),
        scratch_shapes=[pltpu.VMEM((2 * A_VDIM, 2 * A_KDIM), F32),
                        pltpu.VMEM((2, ts, A_KDIM), F32),
                        pltpu.VMEM((2, ts, A_KDIM), F32),
                        pltpu.VMEM((2, ts, A_KDIM), F32),
                        pltpu.VMEM((A_CHUNK, A_VDIM), F32)],
        compiler_params=_cparams(("parallel", "parallel", "arbitrary")),
        name="hgrn2",
    )(mix3, mix3, mix3, mix3, lb, ng)


def _sb_kernel(q_ref, k_ref, v_ref, o_ref, run_ref, acc_ref):
    blk, d = B_BLOCK, B_HDIM
    seq = q_ref.shape[1]
    neg_scale = -(d ** -0.5) * LOG2_E

    def iota(shape, axis):
        return lax.broadcasted_iota(jnp.int32, shape, axis)

    strictly_before = iota((blk, 2 * blk), 1) % blk < iota((blk, 2 * blk), 0)
    r2, c2 = iota((2 * blk, 2 * blk), 0), iota((2 * blk, 2 * blk), 1)
    later_and_ones = jnp.logical_or(r2 % blk > c2, c2 >= blk).astype(BF16)
    head0_lanes = iota((blk, 2 * d), 1) < d

    def load_neg_q(q0):
        return (q_ref[0, pl.ds(q0, blk), :] * neg_scale).astype(BF16)

    def block_diag(x):
        zero = jnp.zeros_like(x)
        return jnp.concatenate([jnp.where(head0_lanes, x, zero), jnp.where(head0_lanes, zero, x)], axis=0)

    def logs(neg_z):
        log_1m = jnp.minimum(neg_z, 0.0) - jnp.log(1.0 + jnp.exp2(-jnp.abs(neg_z))) * LOG2_E
        return log_1m, log_1m - neg_z

    def later_sums(log_1m):
        hi, lo = _split_bf16(log_1m)
        per_head = [_dot(jnp.concatenate([hi[:, h * blk:(h + 1) * blk], lo[:, h * blk:(h + 1) * blk]], axis=1),
                         later_and_ones) for h in range(2)]
        within = jnp.concatenate([per_head[0][:, :blk], per_head[1][:, :blk]], axis=1)
        total = jnp.concatenate([per_head[0][:, blk:], per_head[1][:, blk:]], axis=1)
        return within, total

    def qgroup(qg, carry0, edge):
        qbs = [qg * SB_QBLOCKS_PER_ITER + i for i in range(SB_QBLOCKS_PER_ITER)]
        q0s = [pl.multiple_of(qb * blk, blk) for qb in qbs]
        units = [(i, step) for i in range(SB_QBLOCKS_PER_ITER) for step in range(SB_FULL_BLOCKS)]
        keeps, vals, neg_z = {}, {}, {}
        for i, step in units:
            kb = qbs[i] - step
            k0 = pl.multiple_of(jnp.maximum(kb, 0) * blk, blk)
            keeps[i, step] = strictly_before if step == 0 else (kb >= 0 if edge else None)
            vals[i, step] = block_diag(v_ref[0, pl.ds(k0, blk), :].astype(BF16))
            neg_z[i, step] = _dot_nt(load_neg_q(q0s[i]), block_diag(k_ref[0, pl.ds(k0, blk), :].astype(BF16)))
        log_bs, sums = {}, {}
        for u in units:
            log_1m, log_bs[u] = logs(neg_z[u])
            sums[u] = log_1m if keeps[u] is None else jnp.where(keeps[u], log_1m, 0.0)
        for u in units:
            sums[u] = later_sums(sums[u])
        weights, tops = {}, {}
        for i in range(SB_QBLOCKS_PER_ITER):
            run = jnp.zeros((blk, 2 * blk), F32)
            for step in range(SB_FULL_BLOCKS):
                u = (i, step)
                within, total = sums[u]
                a = jnp.exp2(log_bs[u] + within + run)
                weights[u] = (a if keeps[u] is None else jnp.where(keeps[u], a, 0.0)).astype(BF16)
                run = run + total
            run_ref[i] = run
            tops[i] = jnp.max(run)
        for i in range(SB_QBLOCKS_PER_ITER):
            acc = _dot(weights[i, 0], vals[i, 0])
            for step in range(1, SB_FULL_BLOCKS):
                acc = acc + _dot(weights[i, step], vals[i, step])
            acc_ref[i] = acc

        def cond(st):
            kb, top = st
            return jnp.logical_and(kb >= 0, top > SB_SKIP_LOG * LOG2_E)

        for i in range(SB_QBLOCKS_PER_ITER):
            def body(st, i=i):
                kb, _ = st
                k0 = pl.multiple_of(kb * blk, blk)
                log_1m, log_b = logs(_dot_nt(load_neg_q(q0s[i]),
                                             block_diag(k_ref[0, pl.ds(k0, blk), :].astype(BF16))))
                within, total = later_sums(log_1m)
                run = run_ref[i]
                a = jnp.exp2(log_b + within + run)
                acc_ref[i] = acc_ref[i] + _dot(a.astype(BF16), block_diag(v_ref[0, pl.ds(k0, blk), :].astype(BF16)))
                run = run + total
                run_ref[i] = run
                return kb - 1, jnp.max(run)

            lax.while_loop(cond, body, (qbs[i] - SB_FULL_BLOCKS, tops[i]))
            o_ref[0, pl.ds(q0s[i], blk), :] = acc_ref[i].astype(o_ref.dtype)
        return carry0

    first_full = -(-(SB_FULL_BLOCKS - 1) // SB_QBLOCKS_PER_ITER)
    for qg in range(first_full):
        qgroup(qg, 0, True)
    lax.fori_loop(first_full, seq // (blk * SB_QBLOCKS_PER_ITER), functools.partial(qgroup, edge=False), 0)


def _stick_breaking(mix3):
    bsz, seq, _ = mix3.shape
    pairs = B_HEADS // 2
    w = 2 * B_HDIM
    return pl.pallas_call(
        _sb_kernel,
        grid=(bsz, pairs),
        in_specs=[pl.BlockSpec((1, seq, w), lambda b, p: (b, 0, OFF_BQ + p)),
                  pl.BlockSpec((1, seq, w), lambda b, p: (b, 0, OFF_BK + p)),
                  pl.BlockSpec((1, seq, w), lambda b, p: (b, 0, OFF_BV + p))],
        out_specs=pl.BlockSpec((1, seq, w), lambda b, p: (b, 0, p)),
        out_shape=jax.ShapeDtypeStruct((bsz, seq, B_W), BF16),
        scratch_shapes=[pltpu.VMEM((SB_QBLOCKS_PER_ITER, B_BLOCK, 2 * B_BLOCK), F32),
                        pltpu.VMEM((SB_QBLOCKS_PER_ITER, B_BLOCK, 2 * B_HDIM), F32)],
        compiler_params=_cparams(("parallel", "parallel")),
        name="stick_breaking",
    )(mix3, mix3, mix3)


def _dil_kernel(*refs, tile, slopes):
    blk, d = C_BLOCK, C_HDIM
    n_groups = len(C_GROUPS)
    o_ref, os_ref, ls_ref, stage_ref = refs[5 * n_groups:]
    pair = pl.program_id(1)
    first_tile = pl.program_id(2) == 0
    scale = d ** -0.5
    qi = lax.broadcasted_iota(jnp.int32, (blk, 4 * blk), 0)
    lane4 = lax.broadcasted_iota(jnp.int32, (blk, 4 * blk), 1)
    kj = lane4 % (2 * blk)
    head0_keys = lane4 < 2 * blk
    delta = qi + blk - kj
    head0_out = lax.broadcasted_iota(jnp.int32, (blk, 2 * d), 1) < d
    all_ones = jnp.ones((2 * blk, 2 * d), BF16)

    for g, (window, dil) in enumerate(C_GROUPS):
        q_ref, k_ref, kp_ref, v_ref, vp_ref = group_refs = refs[5 * g:5 * g + 5]
        inner = min(dil, DIL_MAX_STRIDE)
        outer = dil // inner
        if outer > 1:
            for idx, ref in enumerate(group_refs):
                n = ref.shape[1] // inner
                for r in range(inner):
                    stage_ref[idx, r * n:(r + 1) * n, :] = ref[0, pl.ds(r, n, stride=inner), :]
        window_ok = jnp.logical_and(delta >= 0, delta <= window // dil)
        window_ok_first = jnp.logical_and(window_ok, jnp.logical_or(kj >= blk, jnp.logical_not(first_tile)))
        slope0 = jnp.where(pair == 0, jnp.float32(slopes[g][0]), jnp.float32(slopes[g][2]))
        slope1 = jnp.where(pair == 0, jnp.float32(slopes[g][1]), jnp.float32(slopes[g][3]))
        alibi = jnp.where(head0_keys, slope0, slope1) * (delta * dil).astype(F32)

        def strided(ref, rho, block):
            if outer > 1:
                idx = [r is ref for r in group_refs].index(True)
                start = (rho % inner) * (ref.shape[1] // inner) + rho // inner + outer * block * blk
                return stage_ref[idx, pl.ds(start, blk, stride=outer), :]
            start = rho + block * blk * dil
            return ref[0, pl.ds(start, blk, stride=dil), :] if dil > 1 else ref[0, pl.ds(start, blk), :]

        blocks = [(rho, i) for rho in range(dil) for i in range(tile // (dil * blk))]
        for at in range(0, len(blocks), DIL_BATCH):
            batch = blocks[at:at + DIL_BATCH]
            scores, values = {}, {}
            for rho, i in batch:
                k2 = jnp.concatenate([strided(kp_ref, rho, 0) if i == 0 else strided(k_ref, rho, i - 1),
                                      strided(k_ref, rho, i)], axis=0).astype(BF16)
                v2 = jnp.concatenate([strided(vp_ref, rho, 0) if i == 0 else strided(v_ref, rho, i - 1),
                                      strided(v_ref, rho, i)], axis=0).astype(BF16)
                q = (strided(q_ref, rho, i) * scale).astype(BF16)
                zero = jnp.zeros_like(q)
                scores[rho, i] = jnp.concatenate([_dot_nt(jnp.where(head0_out, q, zero), k2),
                                                  _dot_nt(jnp.where(head0_out, zero, q), k2)], axis=1)
                values[rho, i] = jnp.concatenate([v2, all_ones], axis=1)
            probs, tops = {}, {}
            for rho, i in batch:
                valid = window_ok_first if i == 0 else window_ok
                s = jnp.where(valid, scores[rho, i] - alibi, NEG_BIG)
                mx0 = jnp.max(s[:, :2 * blk], axis=-1, keepdims=True)
                mx1 = jnp.max(s[:, 2 * blk:], axis=-1, keepdims=True)
                probs[rho, i] = jnp.exp(s - jnp.where(head0_keys, mx0, mx1)).astype(BF16)
                tops[rho, i] = jnp.where(head0_out, mx0, mx1)
            for rho, i in batch:
                r0 = _dot(probs[rho, i][:, :2 * blk], values[rho, i])
                r1 = _dot(probs[rho, i][:, 2 * blk:], values[rho, i])
                den = jnp.where(head0_out, r0[:, 2 * d:], r1[:, 2 * d:])
                start = rho + i * blk * dil
                rows = pl.ds(start, blk, stride=dil) if dil > 1 else pl.ds(start, blk)
                os_ref[g, rows, :] = jnp.where(head0_out, r0[:, :2 * d], r1[:, :2 * d]) / den
                ls_ref[g, rows, :] = tops[rho, i] + jnp.log(den)
    lses = [ls_ref[g] for g in range(n_groups)]
    top = functools.reduce(jnp.maximum, lses)
    wts = [jnp.exp(l - top) for l in lses]
    num = functools.reduce(lambda a, b: a + b, [wts[g] * os_ref[g] for g in range(n_groups)])
    o_ref[0] = (num / functools.reduce(lambda a, b: a + b, wts)).astype(o_ref.dtype)


def _dilated(mix3):
    bsz, seq, _ = mix3.shape
    pairs = C_HEADS_PER_GROUP // 2
    w = 2 * C_HDIM
    tile = C_BLOCK * max(dil for _, dil in C_GROUPS)
    all_slopes = _alibi_slopes(C_HEADS)
    slopes = tuple(tuple(all_slopes[g * C_HEADS_PER_GROUP:(g + 1) * C_HEADS_PER_GROUP])
                   for g in range(len(C_GROUPS)))
    in_specs = []
    for g, (_, dil) in enumerate(C_GROUPS):
        back = C_BLOCK * dil
        per = tile // back

        def cur(off, g=g):
            return pl.BlockSpec((1, tile, w), lambda b, p, n: (b, n, off + pairs * g + p))

        def prev(off, g=g, back=back, per=per):
            return pl.BlockSpec((1, back, w), lambda b, p, n: (b, jnp.maximum(n * per - 1, 0), off + pairs * g + p))

        in_specs += [cur(OFF_CQ), cur(OFF_CK), prev(OFF_CK), cur(OFF_CV), prev(OFF_CV)]
    return pl.pallas_call(
        functools.partial(_dil_kernel, tile=tile, slopes=slopes),
        grid=(bsz, pairs, seq // tile),
        in_specs=in_specs,
        out_specs=pl.BlockSpec((1, tile, w), lambda b, p, n: (b, n, p)),
        out_shape=jax.ShapeDtypeStruct((bsz, seq, C_OUT), BF16),
        scratch_shapes=[pltpu.VMEM((len(C_GROUPS), tile, w), F32),
                        pltpu.VMEM((len(C_GROUPS), tile, w), F32),
                        pltpu.VMEM((5, tile, w), F32)],
        compiler_params=_cparams(("parallel", "parallel", "arbitrary")),
        name="dilated",
    )(*([mix3] * (5 * len(C_GROUPS))))


def _mixout_kernel(x_ref, mod_ref, g_ref, ya_ref, yb_ref, yc_ref, gate_ref, wa_ref, wb_ref, wc_ref, wo_ref,
                   out_ref, *, sub):
    d = D_MODEL
    subs = [slice(s * sub, (s + 1) * sub) for s in range(x_ref.shape[0] // sub)]
    merged = []
    for rs in subs:
        m = gate_ref[rs, 0:d].astype(F32) * _dot(ya_ref[rs, :], wa_ref[...])
        m = m + gate_ref[rs, d:2 * d].astype(F32) * _dot(yb_ref[rs, :], wb_ref[...])
        m = m + gate_ref[rs, 2 * d:3 * d].astype(F32) * _dot(yc_ref[rs, :], wc_ref[...])
        merged.append(m.astype(BF16))
    ys = [_dot(m, wo_ref[...]) for m in merged]
    for rs, y in zip(subs, ys):
        out_ref[rs, :] = x_ref[rs, :] + mod_ref[0][2:3] * _rms(y, g_ref[...])


def _mixout(x, mod, g, ya, yb, yc, gates, wa, wb, wc, wo, seq, tm=1024, sub=512):
    t, d = x.shape
    per_b = seq // tm

    def tok(wd):
        return pl.BlockSpec((tm, wd), lambda i: (i, 0))

    return pl.pallas_call(
        functools.partial(_mixout_kernel, sub=sub),
        grid=(t // tm,),
        in_specs=[tok(d), pl.BlockSpec((1, 3, d), lambda i: (i // per_b, 0, 0)), _resident((1, d)),
                  tok(A_V), tok(B_W), tok(C_OUT), tok(GATE_COLS),
                  _resident(wa.shape), _resident(wb.shape), _resident(wc.shape), _resident(wo.shape)],
        out_specs=tok(d),
        out_shape=jax.ShapeDtypeStruct((t, d), F32),
        compiler_params=_cparams(("parallel",)),
        name="mix_out",
    )(x, mod, g, ya, yb, yc, gates, wa, wb, wc, wo)


def kernel(x, c, w_ada, b_ada, norm_g, ffn1_w_in, ffn1_w_out, w_in, hgrn_lb_logits, hgrn_norm_g,
           w_branch_a, w_branch_b, w_branch_c, w_out, ffn2_w_in, ffn2_w_out):
    bsz, seq, d = x.shape
    depth = w_ada.shape[0]
    lb_all = _lower_bounds(hgrn_lb_logits.astype(F32))
    mod = _ada(c, w_ada, b_ada).reshape(depth, bsz, 3, 3, d)
    xt = x.reshape(bsz * seq, d)
    for l in range(depth):
        xt = _ffn_sublayer(xt, mod[l, :, 0], norm_g[l, 0:2], ffn1_w_in[l].astype(BF16),
                           ffn1_w_out[l].astype(BF16), 0.5, seq)
        mix, gates = _inproj(xt, mod[l, :, 1], norm_g[l, 2:3], w_in[l].astype(BF16), seq)
        mix3 = mix.reshape(bsz, seq, MIX_COLS)
        ya = _hgrn(mix3, lb_all[l:l + 1], hgrn_norm_g[l:l + 1]).reshape(bsz * seq, A_V)
        yb = _stick_breaking(mix3).reshape(bsz * seq, B_W)
        yc = _dilated(mix3).reshape(bsz * seq, C_OUT)
        xt = _mixout(xt, mod[l, :, 1], norm_g[l, 3:4], ya, yb, yc, gates,
                     w_branch_a[l].astype(BF16), w_branch_b[l].astype(BF16), w_branch_c[l].astype(BF16),
                     w_out[l].astype(BF16), seq)
        xt = _ffn_sublayer(xt, mod[l, :, 2], norm_g[l, 4:6], ffn2_w_in[l].astype(BF16),
                           ffn2_w_out[l].astype(BF16), 0.5, seq)
    return xt.reshape(bsz, seq, d)
```

```python
import functools
import math

import jax
import jax.numpy as jnp
from jax import lax
from jax.experimental import pallas as pl
from jax.experimental.pallas import tpu as pltpu

D_MODEL = 1024
DEPTH = 4
A_HEADS = 6
A_KDIM = 128
A_VDIM = 64
A_CHUNK = 64
B_HEADS = 6
B_HDIM = 64
B_BLOCK = 128
C_GROUPS = ((128, 1), (512, 4), (2048, 16))
C_HEADS_PER_GROUP = 4
C_HEADS = C_HEADS_PER_GROUP * len(C_GROUPS)
C_HDIM = 64
C_BLOCK = 128
D_FF = 2816
N_BRANCH = 3
EPS = 1e-6
NEG_BIG = -1e30
TINY = 1e-30

A_QK = A_HEADS * A_KDIM
A_V = A_HEADS * A_VDIM
B_W = B_HEADS * B_HDIM
C_W = C_HEADS * C_HDIM
C_OUT = C_HEADS_PER_GROUP * C_HDIM
MIX_COLS = 2 * A_QK + 2 * A_V + 3 * B_W + 3 * C_W
GATE_COLS = N_BRANCH * D_MODEL
IN_COLS = MIX_COLS + GATE_COLS

LANES = 128
MXU_WIDTH = 256
OFF_AQ = 0
OFF_AF = A_QK // LANES
OFF_AI = 2 * A_QK // LANES
OFF_AG = (2 * A_QK + A_V) // LANES
OFF_BQ = (2 * A_QK + 2 * A_V) // LANES
OFF_BK = OFF_BQ + B_W // LANES
OFF_BV = OFF_BK + B_W // LANES
OFF_CQ = OFF_BV + B_W // LANES
OFF_CK = OFF_CQ + C_W // LANES
OFF_CV = OFF_CK + C_W // LANES
MIX_BLOCKS = MIX_COLS // LANES

VMEM_LIMIT = 56 * 1024 * 1024
ADA_COL_BLOCKS = 8
SB_SKIP_LOG = -105.0
LOG2_E = 1.4426950408889634
SB_FULL_BLOCKS = 3
SB_QBLOCKS_PER_ITER = 8
HGRN_BATCH = 32
DIL_MAX_STRIDE = 4
DIL_BATCH = 4
HGRN_SAFE_SPAN = 80.0

F32 = jnp.float32
BF16 = jnp.bfloat16


def _alibi_slopes(n):
    def pow2_slopes(m):
        start = 2.0 ** (-8.0 / m)
        return [start ** (i + 1) for i in range(m)]
    if math.log2(n).is_integer():
        s = pow2_slopes(n)
    else:
        c = 2 ** int(math.floor(math.log2(n)))
        s = pow2_slopes(c) + pow2_slopes(2 * c)[0::2][: n - c]
    return sorted(s, reverse=True)


def _rms(x, g):
    return x * lax.rsqrt(jnp.mean(x * x, axis=-1, keepdims=True) + EPS) * g


def _sigmoid(x):
    return 1.0 / (1.0 + jnp.exp(-x))


def _dot(a, b):
    return jnp.dot(a, b, preferred_element_type=F32)


def _dot_nt(a, b):
    return lax.dot_general(a, b, (((1,), (1,)), ((), ())), preferred_element_type=F32)


def _split_bf16(x):
    hi = x.astype(BF16)
    lo = (x - hi.astype(F32)).astype(BF16)
    return hi, lo


def _cparams(sem):
    return pltpu.CompilerParams(dimension_semantics=sem, vmem_limit_bytes=VMEM_LIMIT)


def _resident(shape):
    nd = len(shape)
    return pl.BlockSpec(shape, lambda *_: (0,) * nd, pipeline_mode=pl.Buffered(1))


def _ada_kernel(c_ref, w_ref, b_ref, o_ref):
    c = c_ref[...]
    ca = c * _sigmoid(c)
    o_ref[0] = jnp.dot(ca, w_ref[0], precision=lax.Precision.HIGHEST,
                       preferred_element_type=F32) + b_ref[0]


def _ada(c, w_ada, b_ada):
    depth, d, n = w_ada.shape
    bsz = c.shape[0]
    tn = n // ADA_COL_BLOCKS
    return pl.pallas_call(
        _ada_kernel,
        grid=(depth, n // tn),
        in_specs=[pl.BlockSpec((bsz, d), lambda l, j: (0, 0)),
                  pl.BlockSpec((1, d, tn), lambda l, j: (l, 0, j)),
                  pl.BlockSpec((1, 1, tn), lambda l, j: (l, 0, j))],
        out_specs=pl.BlockSpec((1, bsz, tn), lambda l, j: (l, 0, j)),
        out_shape=jax.ShapeDtypeStruct((depth, bsz, n), F32),
        compiler_params=_cparams(("parallel", "parallel")),
        name="ada_mod",
    )(c, w_ada, b_ada.reshape(depth, 1, n))


def _lb_kernel(l_ref, o_ref):
    x = l_ref[...]
    e = jnp.exp(x - jnp.max(x, axis=0, keepdims=True))
    p = e / jnp.sum(e, axis=0, keepdims=True)
    depth = x.shape[0]
    run = p[0:1]
    o_ref[0:1, :] = run - p[0:1]
    for l in range(1, depth):
        run = run + p[l:l + 1]
        o_ref[l:l + 1, :] = run - p[0:1]


def _lower_bounds(logits):
    return pl.pallas_call(_lb_kernel, out_shape=jax.ShapeDtypeStruct(logits.shape, F32),
                          name="hgrn_lower_bounds")(logits)


def _ffn_kernel(x_ref, mod_ref, g_ref, win_ref, wout_ref, o_ref, act_ref, *, res_w, ffc, sub):
    m = mod_ref[0]
    d_ff = wout_ref.shape[0]
    subs = [slice(s * sub, (s + 1) * sub) for s in range(x_ref.shape[0] // sub)]
    hb = [(_rms(x_ref[rs, :], g_ref[0:1]) * (1.0 + m[1:2]) + m[0:1]).astype(BF16) for rs in subs]
    for s, rs in enumerate(subs):
        for c in range(d_ff // ffc):
            a = _dot(hb[s], win_ref[:, c * ffc:(c + 1) * ffc])
            b = _dot(hb[s], win_ref[:, d_ff + c * ffc:d_ff + (c + 1) * ffc])
            act_ref[rs, c * ffc:(c + 1) * ffc] = (a * _sigmoid(a) * b).astype(BF16)
    ys = [_dot(act_ref[rs, :], wout_ref[...]) for rs in subs]
    for s, rs in enumerate(subs):
        o_ref[rs, :] = x_ref[rs, :] + res_w * m[2:3] * _rms(ys[s], g_ref[1:2])


def _ffn_sublayer(x, mod, g2, w_in, w_out, res_w, seq, tm=1024, ffc=MXU_WIDTH, sub=256):
    t, d = x.shape
    d_ff = w_out.shape[0]
    per_b = seq // tm
    return pl.pallas_call(
        functools.partial(_ffn_kernel, res_w=res_w, ffc=ffc, sub=sub),
        grid=(t // tm,),
        in_specs=[pl.BlockSpec((tm, d), lambda i: (i, 0)),
                  pl.BlockSpec((1, 3, d), lambda i: (i // per_b, 0, 0)),
                  _resident((2, d)),
                  _resident((d, 2 * d_ff)),
                  _resident((d_ff, d))],
        out_specs=pl.BlockSpec((tm, d), lambda i: (i, 0)),
        out_shape=jax.ShapeDtypeStruct((t, d), F32),
        scratch_shapes=[pltpu.VMEM((tm, d_ff), BF16)],
        compiler_params=_cparams(("parallel",)),
        name="ffn_sublayer",
    )(x, mod, g2, w_in, w_out)


def _inproj_kernel(x_ref, mod_ref, g_ref, w_ref, mix_ref, gate_ref, *, cc, sub):
    m = mod_ref[0]
    subs = [slice(s * sub, (s + 1) * sub) for s in range(x_ref.shape[0] // sub)]
    hb = [(_rms(x_ref[rs, :], g_ref[...]) * (1.0 + m[1:2]) + m[0:1]).astype(BF16) for rs in subs]
    for s, rs in enumerate(subs):
        for lo in range(0, IN_COLS, cc):
            hi = min(lo + cc, IN_COLS)
            z = _dot(hb[s], w_ref[:, lo:hi])
            mix_hi = min(hi, MIX_COLS)
            if lo < mix_hi:
                mix_ref[rs, lo:mix_hi] = z[:, :mix_hi - lo]
            if hi > MIX_COLS:
                g_lo = max(lo, MIX_COLS)
                gate_ref[rs, g_lo - MIX_COLS:hi - MIX_COLS] = _sigmoid(z[:, g_lo - lo:]).astype(BF16)


def _inproj(x, mod, g, w_in, seq, tm=512, cc=MXU_WIDTH, sub=256):
    t, d = x.shape
    per_b = seq // tm
    return pl.pallas_call(
        functools.partial(_inproj_kernel, cc=cc, sub=sub),
        grid=(t // tm,),
        in_specs=[pl.BlockSpec((tm, d), lambda i: (i, 0)),
                  pl.BlockSpec((1, 3, d), lambda i: (i // per_b, 0, 0)),
                  _resident((1, d)),
                  _resident((d, IN_COLS))],
        out_specs=[pl.BlockSpec((tm, MIX_COLS), lambda i: (i, 0)),
                   pl.BlockSpec((tm, GATE_COLS), lambda i: (i, 0))],
        out_shape=[jax.ShapeDtypeStruct((t, MIX_COLS), F32),
                   jax.ShapeDtypeStruct((t, GATE_COLS), BF16)],
        compiler_params=_cparams(("parallel",)),
        name="mix_inproj",
    )(x, mod, g, w_in)


def _hgrn_kernel(q_ref, f_ref, i_ref, g_ref, lb_ref, ng_ref, o_ref,
                 state_ref, qs_ref, ks_ref, bs_ref, oi_ref, *, ts):
    ck, kd, vd = A_CHUNK, A_KDIM, A_VDIM
    nc = ts // ck
    mid = ck // 2 - 1

    @pl.when(pl.program_id(2) == 0)
    def _():
        state_ref[...] = jnp.zeros_like(state_ref)

    def iota(shape, axis):
        return lax.broadcasted_iota(jnp.int32, shape, axis)

    tri2 = (iota((ck, 2 * ck), 0) >= iota((ck, 2 * ck), 1) % ck).astype(BF16)
    row1 = iota((ck, 1), 0)
    ng = ng_ref[...]
    ng2 = jnp.concatenate([ng, ng], axis=1)
    rows_of = lambda c: slice(c * ck, (c + 1) * ck)

    qraw = q_ref[0]
    fr = f_ref[0]
    lb = lb_ref[...]
    qs = qraw * _sigmoid(qraw)
    e = jnp.exp(-jnp.abs(fr))
    big = 1.0 / (1.0 + e)
    small = e * big
    pos = fr >= 0.0
    f = lb + (1.0 - lb) * jnp.where(pos, big, small)
    k = (1.0 - lb) * jnp.where(pos, small, big)
    lf_hi, lf_lo = _split_bf16(jnp.log(jnp.maximum(f, TINY)))
    b = jnp.concatenate(
        [_dot(tri2, jnp.concatenate([lf_hi[rows_of(c)], lf_lo[rows_of(c)]], axis=0)) for c in range(nc)],
        axis=0)
    worst = None
    for c in range(nc):
        b_mid = b[c * ck + mid:c * ck + mid + 1]
        b_end = b[(c + 1) * ck - 1:(c + 1) * ck]
        w = jnp.minimum(b_mid, b_end - b_mid)
        worst = w if worst is None else jnp.minimum(worst, w)
    safe = jnp.min(worst) > -HGRN_SAFE_SPAN

    @pl.when(safe)
    def _():
        key_head0 = iota((ck, 2 * kd), 1) < kd
        val_head0 = iota((ck, 2 * vd), 1) < vd
        causal2 = iota((ck, 2 * ck), 0) >= iota((ck, 2 * ck), 1) % ck
        r4 = iota((4 * vd, 2 * vd), 0) % (2 * vd)
        same_head_sum = ((r4 < vd) == (iota((4 * vd, 2 * vd), 1) < vd)).astype(BF16)

        def block_diag(x, head0):
            zero = jnp.zeros_like(x)
            return jnp.concatenate([jnp.where(head0, x, zero), jnp.where(head0, zero, x)], axis=0)

        sts = [state_ref[0:vd, 0:kd], state_ref[vd:, kd:]]
        zero_block = jnp.zeros((vd, kd), BF16)
        for first in range(0, nc, HGRN_BATCH):
            cs = range(first, min(first + HGRN_BATCH, nc))
            qa, ka, q_dec, k_end, end_decay, vb = {}, {}, {}, {}, {}, {}
            for c in cs:
                bc = b[rows_of(c)]
                b_mid = bc[mid:mid + 1]
                b_end = bc[ck - 1:ck]
                qa_f = qs[rows_of(c)] * jnp.exp(bc - b_mid)
                ka_f = k[rows_of(c)] * jnp.exp(b_mid - bc)
                qa[c] = qa_f.astype(BF16)
                ka[c] = block_diag(ka_f.astype(BF16), key_head0)
                q_dec[c] = (qa_f * jnp.exp(b_mid)).astype(BF16)
                k_end[c] = (ka_f * jnp.exp(b_end - b_mid)).astype(BF16)
                end_decay[c] = jnp.exp(b_end)
                vb[c] = i_ref[0, rows_of(c), :].astype(BF16)
            scores = {c: jnp.where(causal2, _dot_nt(qa[c], ka[c]), 0.0).astype(BF16) for c in cs}
            o_intra = {c: _dot(scores[c], block_diag(vb[c], val_head0)) for c in cs}
            update = {c: lax.dot_general(vb[c], k_end[c], (((0,), (0,)), ((), ())), preferred_element_type=F32)
                      for c in cs}
            state_before = {}
            for c in cs:
                state_before[c] = jnp.concatenate(
                    [jnp.concatenate([sts[0].astype(BF16), zero_block], axis=1),
                     jnp.concatenate([zero_block, sts[1].astype(BF16)], axis=1)], axis=0)
                sts = [end_decay[c][:, 0:kd] * sts[0] + update[c][0:vd, 0:kd],
                       end_decay[c][:, kd:] * sts[1] + update[c][vd:, kd:]]
            outs = {c: o_intra[c] + _dot_nt(q_dec[c], state_before[c]) for c in cs}
            squares = {}
            for c in cs:
                hi, lo = _split_bf16(outs[c] * outs[c])
                squares[c] = _dot(jnp.concatenate([hi, lo], axis=1), same_head_sum)
            for c in cs:
                gate = g_ref[0, rows_of(c), :]
                y = outs[c] * lax.rsqrt(squares[c] * (1.0 / vd) + EPS) * ng2
                o_ref[0, rows_of(c), :] = (y * (gate * _sigmoid(gate))).astype(o_ref.dtype)
        state_ref[0:vd, 0:kd] = sts[0]
        state_ref[vd:, kd:] = sts[1]

    @pl.when(jnp.logical_not(safe))
    def _():
        for j in range(2):
            ksl = slice(j * kd, (j + 1) * kd)
            vsl = slice(j * vd, (j + 1) * vd)
            qs_ref[j] = qs[:, ksl]
            ks_ref[j] = k[:, ksl]
            bs_ref[j] = b[:, ksl]

            def chunk(c, carry):
                r0 = pl.multiple_of(c * ck, ck)
                rows = pl.ds(r0, ck)
                bc = bs_ref[j, rows, :]
                qc = qs_ref[j, rows, :]
                kc = ks_ref[j, rows, :]
                v = i_ref[0, rows, vsl]
                vb = v.astype(BF16)

                def one_row(t, cr):
                    bt = bs_ref[j, pl.ds(r0 + t, 1), :]
                    qt = qs_ref[j, pl.ds(r0 + t, 1), :]
                    dec = jnp.exp(jnp.where(row1 <= t, bt - bc, NEG_BIG))
                    sc = jnp.sum(dec * qt * kc, axis=-1, keepdims=True)
                    oi_ref[pl.ds(t, 1), :] = jnp.sum(sc * v, axis=0, keepdims=True)
                    return cr

                lax.fori_loop(0, ck, one_row, 0)
                b_end = bc[ck - 1:ck]
                st = state_ref[vsl, ksl]
                o = oi_ref[...] + _dot_nt((qc * jnp.exp(bc)).astype(BF16), st.astype(BF16))
                k_end = (kc * jnp.exp(b_end - bc)).astype(BF16)
                state_ref[vsl, ksl] = jnp.exp(b_end) * st + lax.dot_general(
                    vb, k_end, (((0,), (0,)), ((), ())), preferred_element_type=F32)
                gate = g_ref[0, rows, vsl]
                o_ref[0, rows, vsl] = (_rms(o, ng) * (gate * _sigmoid(gate))).astype(o_ref.dtype)
                return carry

            lax.fori_loop(0, nc, chunk, 0)


def _hgrn(mix3, lb, ng, ts=2048):
    bsz, seq, _ = mix3.shape
    pairs = A_HEADS // 2
    kw, vw = 2 * A_KDIM, 2 * A_VDIM
    return pl.pallas_call(
        functools.partial(_hgrn_kernel, ts=ts),
        grid=(bsz, pairs, seq // ts),
        in_specs=[pl.BlockSpec((1, ts, kw), lambda b, p, s: (b, s, OFF_AQ * LANES // kw + p)),
                  pl.BlockSpec((1, ts, kw), lambda b, p, s: (b, s, OFF_AF * LANES // kw + p)),
                  pl.BlockSpec((1, ts, vw), lambda b, p, s: (b, s, OFF_AI + p)),
                  pl.BlockSpec((1, ts, vw), lambda b, p, s: (b, s, OFF_AG + p)),
                  pl.BlockSpec((1, kw), lambda b, p, s: (0, p)),
                  pl.BlockSpec((1, A_VDIM), lambda b, p, s: (0, 0))],
        out_specs=pl.BlockSpec((1, ts, vw), lambda b, p, s: (b, s, p)),
        out_shape=jax.ShapeDtypeStruct((bsz, seq, A_V), BF16),
        scratch_shapes=[pltpu.VMEM((2 * A_VDIM, 2 * A_KDIM), F32),
                        pltpu.VMEM((2, ts, A_KDIM), F32),
                        pltpu.VMEM((2, ts, A_KDIM), F32),
                        pltpu.VMEM((2, ts, A_KDIM), F32),
                        pltpu.VMEM((A_CHUNK, A_VDIM), F32)],
        compiler_params=_cparams(("parallel", "parallel", "arbitrary")),
        name="hgrn2",
    )(mix3, mix3, mix3, mix3, lb, ng)


def _sb_kernel(q_ref, k_ref, v_ref, o_ref, run_ref, acc_ref):
    blk, d = B_BLOCK, B_HDIM
    seq = q_ref.shape[1]
    neg_scale = -(d ** -0.5) * LOG2_E

    def iota(shape, axis):
        return lax.broadcasted_iota(jnp.int32, shape, axis)

    strictly_before = iota((blk, 2 * blk), 1) % blk < iota((blk, 2 * blk), 0)
    r2, c2 = iota((2 * blk, 2 * blk), 0), iota((2 * blk, 2 * blk), 1)
    later_and_ones = jnp.logical_or(r2 % blk > c2, c2 >= blk).astype(BF16)
    head0_lanes = iota((blk, 2 * d), 1) < d

    def load_neg_q(q0):
        return (q_ref[0, pl.ds(q0, blk), :] * neg_scale).astype(BF16)

    def block_diag(x):
        zero = jnp.zeros_like(x)
        return jnp.concatenate([jnp.where(head0_lanes, x, zero), jnp.where(head0_lanes, zero, x)], axis=0)

    def logs(neg_z):
        log_1m = jnp.minimum(neg_z, 0.0) - jnp.log(1.0 + jnp.exp2(-jnp.abs(neg_z))) * LOG2_E
        return log_1m, log_1m - neg_z

    def later_sums(log_1m):
        hi, lo = _split_bf16(log_1m)
        per_head = [_dot(jnp.concatenate([hi[:, h * blk:(h + 1) * blk], lo[:, h * blk:(h + 1) * blk]], axis=1),
                         later_and_ones) for h in range(2)]
        within = jnp.concatenate([per_head[0][:, :blk], per_head[1][:, :blk]], axis=1)
        total = jnp.concatenate([per_head[0][:, blk:], per_head[1][:, blk:]], axis=1)
        return within, total

    def qgroup(qg, carry0, edge):
        qbs = [qg * SB_QBLOCKS_PER_ITER + i for i in range(SB_QBLOCKS_PER_ITER)]
        q0s = [pl.multiple_of(qb * blk, blk) for qb in qbs]
        units = [(i, step) for i in range(SB_QBLOCKS_PER_ITER) for step in range(SB_FULL_BLOCKS)]
        keeps, vals, neg_z = {}, {}, {}
        for i, step in units:
            kb = qbs[i] - step
            k0 = pl.multiple_of(jnp.maximum(kb, 0) * blk, blk)
            keeps[i, step] = strictly_before if step == 0 else (kb >= 0 if edge else None)
            vals[i, step] = block_diag(v_ref[0, pl.ds(k0, blk), :].astype(BF16))
            neg_z[i, step] = _dot_nt(load_neg_q(q0s[i]), block_diag(k_ref[0, pl.ds(k0, blk), :].astype(BF16)))
        log_bs, sums = {}, {}
        for u in units:
            log_1m, log_bs[u] = logs(neg_z[u])
            sums[u] = log_1m if keeps[u] is None else jnp.where(keeps[u], log_1m, 0.0)
        for u in units:
            sums[u] = later_sums(sums[u])
        weights, tops = {}, {}
        for i in range(SB_QBLOCKS_PER_ITER):
            run = jnp.zeros((blk, 2 * blk), F32)
            for step in range(SB_FULL_BLOCKS):
                u = (i, step)
                within, total = sums[u]
                a = jnp.exp2(log_bs[u] + within + run)
                weights[u] = (a if keeps[u] is None else jnp.where(keeps[u], a, 0.0)).astype(BF16)
                run = run + total
            run_ref[i] = run
            tops[i] = jnp.max(run)
        for i in range(SB_QBLOCKS_PER_ITER):
            acc = _dot(weights[i, 0], vals[i, 0])
            for step in range(1, SB_FULL_BLOCKS):
                acc = acc + _dot(weights[i, step], vals[i, step])
            acc_ref[i] = acc

        def cond(st):
            kb, top = st
            return jnp.logical_and(kb >= 0, top > SB_SKIP_LOG * LOG2_E)

        for i in range(SB_QBLOCKS_PER_ITER):
            def body(st, i=i):
                kb, _ = st
                k0 = pl.multiple_of(kb * blk, blk)
                log_1m, log_b = logs(_dot_nt(load_neg_q(q0s[i]),
                                             block_diag(k_ref[0, pl.ds(k0, blk), :].astype(BF16))))
                within, total = later_sums(log_1m)
                run = run_ref[i]
                a = jnp.exp2(log_b + within + run)
                acc_ref[i] = acc_ref[i] + _dot(a.astype(BF16), block_diag(v_ref[0, pl.ds(k0, blk), :].astype(BF16)))
                run = run + total
                run_ref[i] = run
                return kb - 1, jnp.max(run)

            lax.while_loop(cond, body, (qbs[i] - SB_FULL_BLOCKS, tops[i]))
            o_ref[0, pl.ds(q0s[i], blk), :] = acc_ref[i].astype(o_ref.dtype)
        return carry0

    first_full = -(-(SB_FULL_BLOCKS - 1) // SB_QBLOCKS_PER_ITER)
    for qg in range(first_full):
        qgroup(qg, 0, True)
    lax.fori_loop(first_full, seq // (blk * SB_QBLOCKS_PER_ITER), functools.partial(qgroup, edge=False), 0)


def _stick_breaking(mix3):
    bsz, seq, _ = mix3.shape
    pairs = B_HEADS // 2
    w = 2 * B_HDIM
    return pl.pallas_call(
        _sb_kernel,
        grid=(bsz, pairs),
        in_specs=[pl.BlockSpec((1, seq, w), lambda b, p: (b, 0, OFF_BQ + p)),
                  pl.BlockSpec((1, seq, w), lambda b, p: (b, 0, OFF_BK + p)),
                  pl.BlockSpec((1, seq, w), lambda b, p: (b, 0, OFF_BV + p))],
        out_specs=pl.BlockSpec((1, seq, w), lambda b, p: (b, 0, p)),
        out_shape=jax.ShapeDtypeStruct((bsz, seq, B_W), BF16),
        scratch_shapes=[pltpu.VMEM((SB_QBLOCKS_PER_ITER, B_BLOCK, 2 * B_BLOCK), F32),
                        pltpu.VMEM((SB_QBLOCKS_PER_ITER, B_BLOCK, 2 * B_HDIM), F32)],
        compiler_params=_cparams(("parallel", "parallel")),
        name="stick_breaking",
    )(mix3, mix3, mix3)


def _dil_kernel(*refs, tile, slopes):
    blk, d = C_BLOCK, C_HDIM
    n_groups = len(C_GROUPS)
    o_ref, os_ref, ls_ref, stage_ref = refs[5 * n_groups:]
    pair = pl.program_id(1)
    first_tile = pl.program_id(2) == 0
    scale = d ** -0.5
    qi = lax.broadcasted_iota(jnp.int32, (blk, 4 * blk), 0)
    lane4 = lax.broadcasted_iota(jnp.int32, (blk, 4 * blk), 1)
    kj = lane4 % (2 * blk)
    head0_keys = lane4 < 2 * blk
    delta = qi + blk - kj
    head0_out = lax.broadcasted_iota(jnp.int32, (blk, 2 * d), 1) < d
    all_ones = jnp.ones((2 * blk, 2 * d), BF16)

    for g, (window, dil) in enumerate(C_GROUPS):
        q_ref, k_ref, kp_ref, v_ref, vp_ref = group_refs = refs[5 * g:5 * g + 5]
        inner = min(dil, DIL_MAX_STRIDE)
        outer = dil // inner
        if outer > 1:
            for idx, ref in enumerate(group_refs):
                n = ref.shape[1] // inner
                for r in range(inner):
                    stage_ref[idx, r * n:(r + 1) * n, :] = ref[0, pl.ds(r, n, stride=inner), :]
        window_ok = jnp.logical_and(delta >= 0, delta <= window // dil)
        window_ok_first = jnp.logical_and(window_ok, jnp.logical_or(kj >= blk, jnp.logical_not(first_tile)))
        slope0 = jnp.where(pair == 0, jnp.float32(slopes[g][0]), jnp.float32(slopes[g][2]))
        slope1 = jnp.where(pair == 0, jnp.float32(slopes[g][1]), jnp.float32(slopes[g][3]))
        alibi = jnp.where(head0_keys, slope0, slope1) * (delta * dil).astype(F32)

        def strided(ref, rho, block):
            if outer > 1:
                idx = [r is ref for r in group_refs].index(True)
                start = (rho % inner) * (ref.shape[1] // inner) + rho // inner + outer * block * blk
                return stage_ref[idx, pl.ds(start, blk, stride=outer), :]
            start = rho + block * blk * dil
            return ref[0, pl.ds(start, blk, stride=dil), :] if dil > 1 else ref[0, pl.ds(start, blk), :]

        blocks = [(rho, i) for rho in range(dil) for i in range(tile // (dil * blk))]
        for at in range(0, len(blocks), DIL_BATCH):
            batch = blocks[at:at + DIL_BATCH]
            scores, values = {}, {}
            for rho, i in batch:
                k2 = jnp.concatenate([strided(kp_ref, rho, 0) if i == 0 else strided(k_ref, rho, i - 1),
                                      strided(k_ref, rho, i)], axis=0).astype(BF16)
                v2 = jnp.concatenate([strided(vp_ref, rho, 0) if i == 0 else strided(v_ref, rho, i - 1),
                                      strided(v_ref, rho, i)], axis=0).astype(BF16)
                q = (strided(q_ref, rho, i) * scale).astype(BF16)
                zero = jnp.zeros_like(q)
                scores[rho, i] = jnp.concatenate([_dot_nt(jnp.where(head0_out, q, zero), k2),
                                                  _dot_nt(jnp.where(head0_out, zero, q), k2)], axis=1)
                values[rho, i] = jnp.concatenate([v2, all_ones], axis=1)
            probs, tops = {}, {}
            for rho, i in batch:
                valid = window_ok_first if i == 0 else window_ok
                s = jnp.where(valid, scores[rho, i] - alibi, NEG_BIG)
                mx0 = jnp.max(s[:, :2 * blk], axis=-1, keepdims=True)
                mx1 = jnp.max(s[:, 2 * blk:], axis=-1, keepdims=True)
                probs[rho, i] = jnp.exp(s - jnp.where(head0_keys, mx0, mx1)).astype(BF16)
                tops[rho, i] = jnp.where(head0_out, mx0, mx1)
            for rho, i in batch:
                r0 = _dot(probs[rho, i][:, :2 * blk], values[rho, i])
                r1 = _dot(probs[rho, i][:, 2 * blk:], values[rho, i])
                den = jnp.where(head0_out, r0[:, 2 * d:], r1[:, 2 * d:])
                start = rho + i * blk * dil
                rows = pl.ds(start, blk, stride=dil) if dil > 1 else pl.ds(start, blk)
                os_ref[g, rows, :] = jnp.where(head0_out, r0[:, :2 * d], r1[:, :2 * d]) / den
                ls_ref[g, rows, :] = tops[rho, i] + jnp.log(den)
    lses = [ls_ref[g] for g in range(n_groups)]
    top = functools.reduce(jnp.maximum, lses)
    wts = [jnp.exp(l - top) for l in lses]
    num = functools.reduce(lambda a, b: a + b, [wts[g] * os_ref[g] for g in range(n_groups)])
    o_ref[0] = (num / functools.reduce(lambda a, b: a + b, wts)).astype(o_ref.dtype)


def _dilated(mix3):
    bsz, seq, _ = mix3.shape
    pairs = C_HEADS_PER_GROUP // 2
    w = 2 * C_HDIM
    tile = C_BLOCK * max(dil for _, dil in C_GROUPS)
    all_slopes = _alibi_slopes(C_HEADS)
    slopes = tuple(tuple(all_slopes[g * C_HEADS_PER_GROUP:(g + 1) * C_HEADS_PER_GROUP])
                   for g in range(len(C_GROUPS)))
    in_specs = []
    for g, (_, dil) in enumerate(C_GROUPS):
        back = C_BLOCK * dil
        per = tile // back

        def cur(off, g=g):
            return pl.BlockSpec((1, tile, w), lambda b, p, n: (b, n, off + pairs * g + p))

        def prev(off, g=g, back=back, per=per):
            return pl.BlockSpec((1, back, w), lambda b, p, n: (b, jnp.maximum(n * per - 1, 0), off + pairs * g + p))

        in_specs += [cur(OFF_CQ), cur(OFF_CK), prev(OFF_CK), cur(OFF_CV), prev(OFF_CV)]
    return pl.pallas_call(
        functools.partial(_dil_kernel, tile=tile, slopes=slopes),
        grid=(bsz, pairs, seq // tile),
        in_specs=in_specs,
        out_specs=pl.BlockSpec((1, tile, w), lambda b, p, n: (b, n, p)),
        out_shape=jax.ShapeDtypeStruct((bsz, seq, C_OUT), BF16),
        scratch_shapes=[pltpu.VMEM((len(C_GROUPS), tile, w), F32),
                        pltpu.VMEM((len(C_GROUPS), tile, w), F32),
                        pltpu.VMEM((5, tile, w), F32)],
        compiler_params=_cparams(("parallel", "parallel", "arbitrary")),
        name="dilated",
    )(*([mix3] * (5 * len(C_GROUPS))))


def _mixout_kernel(x_ref, mod_ref, g_ref, ya_ref, yb_ref, yc_ref, gate_ref, wa_ref, wb_ref, wc_ref, wo_ref,
                   out_ref, *, sub):
    d = D_MODEL
    subs = [slice(s * sub, (s + 1) * sub) for s in range(x_ref.shape[0] // sub)]
    merged = []
    for rs in subs:
        m = gate_ref[rs, 0:d].astype(F32) * _dot(ya_ref[rs, :], wa_ref[...])
        m = m + gate_ref[rs, d:2 * d].astype(F32) * _dot(yb_ref[rs, :], wb_ref[...])
        m = m + gate_ref[rs, 2 * d:3 * d].astype(F32) * _dot(yc_ref[rs, :], wc_ref[...])
        merged.append(m.astype(BF16))
    ys = [_dot(m, wo_ref[...]) for m in merged]
    for rs, y in zip(subs, ys):
        out_ref[rs, :] = x_ref[rs, :] + mod_ref[0][2:3] * _rms(y, g_ref[...])


def _mixout(x, mod, g, ya, yb, yc, gates, wa, wb, wc, wo, seq, tm=1024, sub=512):
    t, d = x.shape
    per_b = seq // tm

    def tok(wd):
        return pl.BlockSpec((tm, wd), lambda i: (i, 0))

    return pl.pallas_call(
        functools.partial(_mixout_kernel, sub=sub),
        grid=(t // tm,),
        in_specs=[tok(d), pl.BlockSpec((1, 3, d), lambda i: (i // per_b, 0, 0)), _resident((1, d)),
                  tok(A_V), tok(B_W), tok(C_OUT), tok(GATE_COLS),
                  _resident(wa.shape), _resident(wb.shape), _resident(wc.shape), _resident(wo.shape)],
        out_specs=tok(d),
        out_shape=jax.ShapeDtypeStruct((t, d), F32),
        compiler_params=_cparams(("parallel",)),
        name="mix_out",
    )(x, mod, g, ya, yb, yc, gates, wa, wb, wc, wo)


def _mixffn_kernel(x_ref, mod_ref, g_ref, ya_ref, yb_ref, yc_ref, gate_ref, wa_ref, wb_ref, wc_ref, wo_ref,
                   win_ref, wout_ref, out_ref, act_ref, *, sub, ffc, res_w):
    d = D_MODEL
    d_ff = wout_ref.shape[0]
    m_mix, m_ffn = mod_ref[0, 0], mod_ref[0, 1]
    subs = [slice(s * sub, (s + 1) * sub) for s in range(x_ref.shape[0] // sub)]
    merged = []
    for rs in subs:
        m = gate_ref[rs, 0:d].astype(F32) * _dot(ya_ref[rs, :], wa_ref[...])
        m = m + gate_ref[rs, d:2 * d].astype(F32) * _dot(yb_ref[rs, :], wb_ref[...])
        m = m + gate_ref[rs, 2 * d:3 * d].astype(F32) * _dot(yc_ref[rs, :], wc_ref[...])
        merged.append(m.astype(BF16))
    ys = [_dot(m, wo_ref[...]) for m in merged]
    xs, hb = [], []
    for rs, y in zip(subs, ys):
        x1 = x_ref[rs, :] + m_mix[2:3] * _rms(y, g_ref[0:1])
        xs.append(x1)
        hb.append((_rms(x1, g_ref[1:2]) * (1.0 + m_ffn[1:2]) + m_ffn[0:1]).astype(BF16))
    for s, rs in enumerate(subs):
        for c in range(d_ff // ffc):
            a = _dot(hb[s], win_ref[:, c * ffc:(c + 1) * ffc])
            b = _dot(hb[s], win_ref[:, d_ff + c * ffc:d_ff + (c + 1) * ffc])
            act_ref[rs, c * ffc:(c + 1) * ffc] = (a * _sigmoid(a) * b).astype(BF16)
    y2 = [_dot(act_ref[rs, :], wout_ref[...]) for rs in subs]
    for s, rs in enumerate(subs):
        out_ref[rs, :] = xs[s] + res_w * m_ffn[2:3] * _rms(y2[s], g_ref[2:3])


def _mixffn(x, mod2, g3, ya, yb, yc, gates, wa, wb, wc, wo, w_in, w_out, res_w, seq,
            tm=512, sub=256, ffc=MXU_WIDTH):
    t, d = x.shape
    d_ff = w_out.shape[0]
    per_b = seq // tm

    def tok(wd):
        return pl.BlockSpec((tm, wd), lambda i: (i, 0))

    return pl.pallas_call(
        functools.partial(_mixffn_kernel, sub=sub, ffc=ffc, res_w=res_w),
        grid=(t // tm,),
        in_specs=[tok(d), pl.BlockSpec((1, 2, 3, d), lambda i: (i // per_b, 0, 0, 0)), _resident((3, d)),
                  tok(A_V), tok(B_W), tok(C_OUT), tok(GATE_COLS),
                  _resident(wa.shape), _resident(wb.shape), _resident(wc.shape), _resident(wo.shape),
                  _resident((d, 2 * d_ff)), _resident((d_ff, d))],
        out_specs=tok(d),
        out_shape=jax.ShapeDtypeStruct((t, d), F32),
        scratch_shapes=[pltpu.VMEM((tm, d_ff), BF16)],
        compiler_params=_cparams(("parallel",)),
        name="mix_out_ffn",
    )(x, mod2, g3, ya, yb, yc, gates, wa, wb, wc, wo, w_in, w_out)


def kernel(x, c, w_ada, b_ada, norm_g, ffn1_w_in, ffn1_w_out, w_in, hgrn_lb_logits, hgrn_norm_g,
           w_branch_a, w_branch_b, w_branch_c, w_out, ffn2_w_in, ffn2_w_out):
    bsz, seq, d = x.shape
    depth = w_ada.shape[0]
    lb_all = _lower_bounds(hgrn_lb_logits.astype(F32))
    mod = _ada(c, w_ada, b_ada).reshape(depth, bsz, 3, 3, d)
    xt = x.reshape(bsz * seq, d)
    for l in range(depth):
        xt = _ffn_sublayer(xt, mod[l, :, 0], norm_g[l, 0:2], ffn1_w_in[l].astype(BF16),
                           ffn1_w_out[l].astype(BF16), 0.5, seq)
        mix, gates = _inproj(xt, mod[l, :, 1], norm_g[l, 2:3], w_in[l].astype(BF16), seq)
        mix3 = mix.reshape(bsz, seq, MIX_COLS)
        ya = _hgrn(mix3, lb_all[l:l + 1], hgrn_norm_g[l:l + 1]).reshape(bsz * seq, A_V)
        yb = _stick_breaking(mix3).reshape(bsz * seq, B_W)
        yc = _dilated(mix3).reshape(bsz * seq, C_OUT)
        xt = _mixffn(xt, mod[l, :, 1:3], norm_g[l, 3:6], ya, yb, yc, gates,
                     w_branch_a[l].astype(BF16), w_branch_b[l].astype(BF16), w_branch_c[l].astype(BF16),
                     w_out[l].astype(BF16), ffn2_w_in[l].astype(BF16), ffn2_w_out[l].astype(BF16), 0.5, seq)
    return xt.reshape(bsz, seq, d)
```
